```python
import math
import jax
import jax.numpy as jnp
from jax import lax
import numpy as np

D_MODEL = 4096
BATCH = 4
SEQ = 2048
DEPTH = 2
DEC_BATCH = 8
DEC_SEQ = 8
PAST_LEN = 16384
PAGE_SIZE = 128

HEAD_DIM = 128
N_MIX_HEADS = D_MODEL // HEAD_DIM
FOX_HEADS = N_MIX_HEADS // 4
SB_HEADS = N_MIX_HEADS // 4
HG_HEADS = N_MIX_HEADS - FOX_HEADS - SB_HEADS
HG_DK = 128
HG_DV = HEAD_DIM
FOX_W = FOX_HEADS * HEAD_DIM
SB_W = SB_HEADS * HEAD_DIM
ATT_W = FOX_W + SB_W
HG_K_W = HG_HEADS * HG_DK
HG_W = HG_HEADS * HG_DV
MIX_W = ATT_W + HG_W
D_FF = 11008
PLE_DIM = 256
Q_BLOCK = 128
HG_CHUNK = 64
EPS = 1e-6
FOX_BIAS_INIT = 4.0
CACHE_LOGF_CENTER = 6.0
COL_SIZES = (FOX_W, FOX_W, FOX_W, FOX_HEADS, SB_W, SB_W, SB_W, HG_K_W, HG_K_W, HG_W, HG_W)
COL_OFFSETS = tuple(int(c) for c in np.cumsum(COL_SIZES)[:-1])
N_IN_COLS = int(sum(COL_SIZES))

kernel_name = 'hybrid_fox_stickbreak_hgrn2_step'


def rms_norm(x, g):
    xf = x.astype(jnp.float32)
    y = xf * lax.rsqrt(jnp.mean(xf * xf, axis=-1, keepdims=True) + EPS)
    return (y * g.astype(jnp.float32)).astype(x.dtype)


def swiglu(x, w1, w3, w2):
    return (jax.nn.silu(x @ w1) * (x @ w3)) @ w2


def rev_cumsum_excl(x):
    if x.shape[1] == 0:
        return x
    rc = lax.cumsum(x, axis=1, reverse=True)
    return jnp.concatenate([rc[:, 1:], jnp.zeros_like(rc[:, :1])], axis=1)


def fox_block(q, k, v, logf_past, logf_new):
    n_past, L = logf_past.shape[1], q.shape[1]
    e = jnp.cumsum(logf_new, axis=1)
    kc = jnp.concatenate([-rev_cumsum_excl(logf_past), e], axis=1)
    mask = jnp.concatenate([jnp.ones((L, n_past), bool), jnp.tri(L, dtype=bool)], axis=1)
    s = jnp.einsum('bqhd,bkhd->bhqk', q, k).astype(jnp.float32) * (q.shape[-1] ** -0.5)
    s = s + jnp.transpose(e, (0, 2, 1))[:, :, :, None] - jnp.transpose(kc, (0, 2, 1))[:, :, None, :]
    p = jax.nn.softmax(jnp.where(mask, s, -jnp.inf), axis=-1)
    return jnp.einsum('bhqk,bkhd->bqhd', p.astype(v.dtype), v)


def sb_block(q, k, v, n_past):
    L = q.shape[1]
    mask = jnp.concatenate([jnp.ones((L, n_past), bool), jnp.tri(L, k=-1, dtype=bool)], axis=1)
    z = jnp.einsum('bqhd,bkhd->bhqk', q, k).astype(jnp.float32) * (q.shape[-1] ** -0.5)
    log_1mb = jnp.where(mask, jax.nn.log_sigmoid(-z), 0.0)
    rc = lax.cumsum(log_1mb, axis=3, reverse=True)
    excl = jnp.concatenate([rc[..., 1:], jnp.zeros_like(rc[..., :1])], axis=-1)
    log_a = jnp.where(mask, jax.nn.log_sigmoid(z) + excl, -jnp.inf)
    return jnp.einsum('bhqk,bkhd->bqhd', jnp.exp(log_a).astype(v.dtype), v)


def fox_prompt(q, k, v, logf):
    outs = []
    for q0 in range(0, q.shape[1], Q_BLOCK):
        q1 = q0 + Q_BLOCK
        outs.append(fox_block(q[:, q0:q1], k[:, :q1], v[:, :q1], logf[:, :q0], logf[:, q0:q1]))
    return jnp.concatenate(outs, axis=1)


def sb_prompt(q, k, v):
    outs = []
    for q0 in range(0, q.shape[1], Q_BLOCK):
        q1 = q0 + Q_BLOCK
        outs.append(sb_block(q[:, q0:q1], k[:, :q1], v[:, :q1], q0))
    return jnp.concatenate(outs, axis=1)


def hgrn_chunked(q, k, logf, inp, s0):
    bsz, L, H, K = q.shape
    C = L if L <= HG_CHUNK else math.gcd(L, HG_CHUNK)
    n = L // C
    tri = jnp.tri(C, dtype=bool)[None, :, :, None, None]

    def to_chunks(a):
        return jnp.moveaxis(a.reshape(bsz, n, C, *a.shape[2:]), 1, 0)

    def step(S, xs):
        qc, kc, gc, ic = xs
        b = jnp.cumsum(gc, axis=1)
        dec = jnp.where(tri, b[:, :, None] - b[:, None, :], -jnp.inf)
        w = jnp.einsum('bthk,btshk,bshk->bhts', qc, jnp.exp(dec), kc)
        o = jnp.einsum('bhts,bshv->bthv', w, ic) + jnp.einsum('bthk,bhkv->bthv', qc * jnp.exp(b), S)
        b_last = b[:, -1]
        S_new = jnp.exp(b_last)[..., None] * S + jnp.einsum('bshk,bshv->bhkv', kc * jnp.exp(b_last[:, None] - b), ic)
        return S_new, o

    s_fin, o = lax.scan(step, s0, (to_chunks(q), to_chunks(k), to_chunks(logf), to_chunks(inp)))
    return jnp.moveaxis(o, 0, 1).reshape(bsz, L, H, inp.shape[-1]), s_fin


def mixer_inputs(h, w_in, f_bias, fq_g, fk_g, sq_g, sk_g, lb):
    bsz, L, _ = h.shape
    fq, fk, fv, ff, sq, sk, sv, hq, hf, hi, hg = jnp.split(h @ w_in, COL_OFFSETS, axis=-1)

    def heads(a, n):
        return a.reshape(bsz, L, n, -1)

    fox = (rms_norm(heads(fq, FOX_HEADS), fq_g), rms_norm(heads(fk, FOX_HEADS), fk_g), heads(fv, FOX_HEADS),
           jax.nn.log_sigmoid((ff + f_bias).astype(jnp.float32)))
    sb = (rms_norm(heads(sq, SB_HEADS), sq_g), rms_norm(heads(sk, SB_HEADS), sk_g), heads(sv, SB_HEADS))
    lb = lb.reshape(HG_HEADS, HG_DK)
    hg_logf = jnp.logaddexp(jnp.log1p(-lb) + jax.nn.log_sigmoid(heads(hf, HG_HEADS).astype(jnp.float32)), jnp.log(lb))
    hgrn = (heads(hq, HG_HEADS).astype(jnp.float32), -jnp.expm1(hg_logf), hg_logf,
            heads(hi, HG_HEADS).astype(jnp.float32), hg)
    return fox, sb, hgrn


def mixer_output(fox_o, sb_o, hg_o, hg_gate, out_g, w_out):
    bsz, L = fox_o.shape[:2]
    o = jnp.concatenate([fox_o, sb_o, hg_o.astype(fox_o.dtype)], axis=2)
    o = rms_norm(o, out_g.reshape(N_MIX_HEADS, HEAD_DIM)).reshape(bsz, L, MIX_W)
    o = jnp.concatenate([o[..., :ATT_W], o[..., ATT_W:] * jax.nn.silu(hg_gate)], axis=-1)
    return o @ w_out


def setup_inputs(seed: int = 0) -> dict:
    key = jax.random.key(seed)
    ks = iter(jax.random.split(key, 40))

    def nrm(shape, scale):
        return jax.random.normal(next(ks), shape, jnp.float32) * scale

    def gain(shape):
        return 1.0 + nrm(shape, 0.05)

    n_pages = PAST_LEN // PAGE_SIZE
    n_pool = (DEC_BATCH * n_pages * 5) // 4
    perm = jax.random.permutation(next(ks), n_pool)
    page_table = perm[:DEC_BATCH * n_pages].reshape(DEC_BATCH, n_pages).astype(jnp.int32)
    return {
        'x_prompt': nrm((BATCH, SEQ, D_MODEL), 1.0),
        'x_sample': nrm((DEC_BATCH, DEC_SEQ, D_MODEL), 1.0),
        'p_prompt': nrm((DEPTH, BATCH, SEQ, PLE_DIM), 1.0),
        'p_sample': nrm((DEPTH, DEC_BATCH, DEC_SEQ, PLE_DIM), 1.0),
        'cache_fox_kv': nrm((n_pool, DEPTH, PAGE_SIZE, FOX_HEADS, 2 * HEAD_DIM), 1.0),
        'cache_fox_logf': jax.nn.log_sigmoid(CACHE_LOGF_CENTER + nrm((n_pool, DEPTH, PAGE_SIZE, FOX_HEADS), 0.5)),
        'cache_sb_kv': nrm((n_pool, DEPTH, PAGE_SIZE, SB_HEADS, 2 * HEAD_DIM), 1.0),
        'state_hgrn': nrm((DEPTH, DEC_BATCH, HG_HEADS, HG_DK, HG_DV), 0.3),
        'page_table': page_table,
        'ffn1_norm': gain((DEPTH, D_MODEL)),
        'ffn1_w1': nrm((DEPTH, D_MODEL, D_FF), D_MODEL ** -0.5),
        'ffn1_w3': nrm((DEPTH, D_MODEL, D_FF), D_MODEL ** -0.5),
        'ffn1_w2': nrm((DEPTH, D_FF, D_MODEL), D_FF ** -0.5),
        'mix_norm': gain((DEPTH, D_MODEL)),
        'w_in': nrm((DEPTH, D_MODEL, N_IN_COLS), D_MODEL ** -0.5),
        'fox_f_bias': FOX_BIAS_INIT + nrm((DEPTH, FOX_HEADS), 0.5),
        'fox_q_norm': gain((DEPTH, HEAD_DIM)),
        'fox_k_norm': gain((DEPTH, HEAD_DIM)),
        'sb_q_norm': gain((DEPTH, HEAD_DIM)),
        'sb_k_norm': gain((DEPTH, HEAD_DIM)),
        'hg_lower_bound': nrm((DEPTH, HG_K_W), 0.5),
        'out_norm': gain((DEPTH, MIX_W)),
        'w_out': nrm((DEPTH, MIX_W, D_MODEL), MIX_W ** -0.5),
        'ffn2_norm': gain((DEPTH, D_MODEL)),
        'ffn2_w1': nrm((DEPTH, D_MODEL, D_FF), D_MODEL ** -0.5),
        'ffn2_w3': nrm((DEPTH, D_MODEL, D_FF), D_MODEL ** -0.5),
        'ffn2_w2': nrm((DEPTH, D_FF, D_MODEL), D_FF ** -0.5),
        'ple_norm': gain((DEPTH, D_MODEL)),
        'w_ple_gate': nrm((DEPTH, D_MODEL, D_MODEL), D_MODEL ** -0.5),
        'w_ple_proj': nrm((DEPTH, PLE_DIM, D_MODEL), PLE_DIM ** -0.5),
    }


def reference(x_prompt, x_sample, p_prompt, p_sample, cache_fox_kv, cache_fox_logf, cache_sb_kv, state_hgrn,
              page_table, ffn1_norm, ffn1_w1, ffn1_w3, ffn1_w2, mix_norm, w_in, fox_f_bias, fox_q_norm,
              fox_k_norm, sb_q_norm, sb_k_norm, hg_lower_bound, out_norm, w_out, ffn2_norm, ffn2_w1, ffn2_w3,
              ffn2_w2, ple_norm, w_ple_gate, w_ple_proj):
    gamma = jnp.cumsum(jax.nn.softmax(hg_lower_bound.astype(jnp.float32), axis=0), axis=0)

    def pre_mix(x, l):
        x = x + 0.5 * swiglu(rms_norm(x, ffn1_norm[l]), ffn1_w1[l], ffn1_w3[l], ffn1_w2[l])
        lb = gamma[l - 1] if l > 0 else jnp.zeros_like(gamma[0])
        feats = mixer_inputs(rms_norm(x, mix_norm[l]), w_in[l], fox_f_bias[l], fox_q_norm[l], fox_k_norm[l],
                             sb_q_norm[l], sb_k_norm[l], lb)
        return x, feats

    def post_mix(x, l, fox_o, sb_o, hg_o, hg_gate, p_l):
        x = x + mixer_output(fox_o, sb_o, hg_o, hg_gate, out_norm[l], w_out[l])
        x = x + 0.5 * swiglu(rms_norm(x, ffn2_norm[l]), ffn2_w1[l], ffn2_w3[l], ffn2_w2[l])
        gate = jax.nn.sigmoid(rms_norm(x, ple_norm[l]) @ w_ple_gate[l])
        return x + gate * (p_l @ w_ple_proj[l])

    x = x_prompt
    pf_kv, pf_logf, psb_kv, phg = [], [], [], []
    for l in range(DEPTH):
        x, (fox, sb, hg) = pre_mix(x, l)
        fq, fk, fv, flogf = fox
        sq, sk, sv = sb
        hq, hk, hlogf, hi, hgate = hg
        fox_o = fox_prompt(fq, fk, fv, flogf)
        sb_o = sb_prompt(sq, sk, sv)
        s0 = jnp.zeros((x.shape[0], HG_HEADS, HG_DK, HG_DV), jnp.float32)
        hg_o, hg_s = hgrn_chunked(hq, hk, hlogf, hi, s0)
        x = post_mix(x, l, fox_o, sb_o, hg_o, hgate, p_prompt[l])
        pf_kv.append(jnp.concatenate([fk, fv], axis=-1))
        pf_logf.append(flogf)
        psb_kv.append(jnp.concatenate([sk, sv], axis=-1))
        phg.append(hg_s)
    y_prompt = x

    n_past = page_table.shape[1] * PAGE_SIZE

    def gather(cache, l):
        rows = cache[page_table, l]
        return rows.reshape(rows.shape[0], n_past, *rows.shape[3:])

    x = x_sample
    sf_kv, sf_logf, ssb_kv, shg = [], [], [], []
    for l in range(DEPTH):
        x, (fox, sb, hg) = pre_mix(x, l)
        fq, fk, fv, flogf = fox
        sq, sk, sv = sb
        hq, hk, hlogf, hi, hgate = hg
        fkv_past = gather(cache_fox_kv, l)
        flogf_past = gather(cache_fox_logf, l).astype(jnp.float32)
        fox_o = fox_block(fq, jnp.concatenate([fkv_past[..., :HEAD_DIM], fk], axis=1),
                          jnp.concatenate([fkv_past[..., HEAD_DIM:], fv], axis=1), flogf_past, flogf)
        skv_past = gather(cache_sb_kv, l)
        sb_o = sb_block(sq, jnp.concatenate([skv_past[..., :HEAD_DIM], sk], axis=1),
                        jnp.concatenate([skv_past[..., HEAD_DIM:], sv], axis=1), n_past)
        hg_o, hg_s = hgrn_chunked(hq, hk, hlogf, hi, state_hgrn[l].astype(jnp.float32))
        x = post_mix(x, l, fox_o, sb_o, hg_o, hgate, p_sample[l])
        sf_kv.append(jnp.concatenate([fk, fv], axis=-1))
        sf_logf.append(flogf)
        ssb_kv.append(jnp.concatenate([sk, sv], axis=-1))
        shg.append(hg_s)
    y_sample = x

    fox_kv_prompt = jnp.stack(pf_kv, axis=1)
    fox_logf_prompt = jnp.stack(pf_logf, axis=1)
    sb_kv_prompt = jnp.stack(psb_kv, axis=1)
    hgrn_state_prompt = jnp.stack(phg, axis=0)
    fox_kv_sample = jnp.stack(sf_kv, axis=1)
    fox_logf_sample = jnp.stack(sf_logf, axis=1)
    sb_kv_sample = jnp.stack(ssb_kv, axis=1)
    hgrn_state_sample = jnp.stack(shg, axis=0)
    return (y_prompt, y_sample, fox_kv_prompt, fox_logf_prompt, sb_kv_prompt, hgrn_state_prompt,
            fox_kv_sample, fox_logf_sample, sb_kv_sample, hgrn_state_sample)
```

```python
import functools

import jax
import jax.numpy as jnp
from jax import lax
from jax.experimental import pallas as pl
from jax.experimental.pallas import tpu as pltpu

F32 = jnp.float32
BF16 = jnp.bfloat16

HEAD_DIM = 128
EPS = 1e-6
LANES = 128
SUBLANES = 8
VMEM_LIMIT_BYTES = 56 << 20
ROW_TILE = 512
COL_TILE = 1024
FF_TILE = 256
FOX_BLOCK = 256
SB_BLOCK = 128
HG_CHUNK = 128
HG_HEADS_PER_STEP = 4


def _cparams(*sem):
    return pltpu.CompilerParams(dimension_semantics=sem, vmem_limit_bytes=VMEM_LIMIT_BYTES)


def _divisor_tile(n, pref, align=LANES):
    if n <= pref:
        return n
    for t in range(pref - pref % align, 0, -align):
        if n % t == 0:
            return t
    raise ValueError((n, pref, align))


def _dot(a, b):
    return jnp.dot(a, b, preferred_element_type=F32)


def _dot_nt(a, b):
    return lax.dot_general(a, b, (((1,), (1,)), ((), ())), preferred_element_type=F32)


def _dot_tn(a, b):
    return lax.dot_general(a, b, (((0,), (0,)), ((), ())), preferred_element_type=F32)


def _split3(x):
    hi = x.astype(BF16)
    r1 = x - hi.astype(F32)
    mid = r1.astype(BF16)
    lo = (r1 - mid.astype(F32)).astype(BF16)
    return hi, mid, lo


def _dot3_lhs(x, m_bf16):
    hi, mid, lo = _split3(x)
    return _dot(hi, m_bf16) + _dot(mid, m_bf16) + _dot(lo, m_bf16)


def _dot3_rhs(m_bf16, x):
    hi, mid, lo = _split3(x)
    return _dot(m_bf16, hi) + _dot(m_bf16, mid) + _dot(m_bf16, lo)


def _log_sigmoid(x):
    return jnp.minimum(x, 0.0) - jnp.log1p(jnp.exp(-jnp.abs(x)))


def _rms_rows(x, gain):
    return x * lax.rsqrt(jnp.mean(x * x, axis=-1, keepdims=True) + EPS) * gain


def _norm_into(x_ref, g_ref, h_ref):
    h_ref[...] = _rms_rows(x_ref[...], g_ref[...]).astype(BF16)


def _ones_where(cond):
    return jnp.where(cond, 1.0, 0.0).astype(BF16)


def _pick_lane(x, idx):
    lane = lax.broadcasted_iota(jnp.int32, x.shape, 1)
    return jnp.sum(jnp.where(lane == idx, x, 0.0), axis=1, keepdims=True)


def _pick_row(x, idx):
    row = lax.broadcasted_iota(jnp.int32, x.shape, 0)
    return jnp.sum(jnp.where(row == idx, x, 0.0), axis=0, keepdims=True)


def _repeat_rows(x, reps):
    return jnp.concatenate(
        [jnp.broadcast_to(x[h:h + 1, :], (reps, x.shape[1])) for h in range(x.shape[0])], axis=0)


def _ffn_kernel(x_ref, g_ref, w1_ref, w3_ref, w2_ref, o_ref, h_ref):
    @pl.when(pl.program_id(1) == 0)
    def _():
        _norm_into(x_ref, g_ref, h_ref)
        o_ref[...] = x_ref[...]

    h = h_ref[...]
    a = _dot(h, w1_ref[...])
    b = _dot(h, w3_ref[...])
    gated = (a * jax.nn.sigmoid(a) * b * 0.5).astype(BF16)
    o_ref[...] += _dot(gated, w2_ref[...])


def _ffn(x, gain, w1, w3, w2):
    T, D = x.shape
    F = w1.shape[1]
    tm = _divisor_tile(T, ROW_TILE)
    tf = _divisor_tile(F, FF_TILE)
    return pl.pallas_call(
        _ffn_kernel,
        grid=(T // tm, F // tf),
        in_specs=[pl.BlockSpec((tm, D), lambda i, j: (i, 0), pipeline_mode=pl.Buffered(1)),
                  pl.BlockSpec((1, D), lambda i, j: (0, 0)),
                  pl.BlockSpec((D, tf), lambda i, j: (0, j)),
                  pl.BlockSpec((D, tf), lambda i, j: (0, j)),
                  pl.BlockSpec((tf, D), lambda i, j: (j, 0))],
        out_specs=pl.BlockSpec((tm, D), lambda i, j: (i, 0)),
        out_shape=jax.ShapeDtypeStruct((T, D), F32),
        scratch_shapes=[pltpu.VMEM((tm, D), BF16)],
        compiler_params=_cparams("parallel", "arbitrary"),
        name="ffn",
    )(x, gain, w1, w3, w2)


def _proj_heads_kernel(x_ref, g_ref, w_ref, hgain_ref, flag_ref, o_ref, h_ref):
    @pl.when(pl.program_id(1) == 0)
    def _():
        _norm_into(x_ref, g_ref, h_ref)

    z = _dot(h_ref[...], w_ref[...])
    for c in range(z.shape[1] // HEAD_DIM):
        sl = slice(c * HEAD_DIM, (c + 1) * HEAD_DIM)
        zc = z[:, sl]
        o_ref[:, sl] = jnp.where(flag_ref[:, sl] > 0.0, _rms_rows(zc, hgain_ref[:, sl]), zc)


def _proj_plain_kernel(x_ref, g_ref, w_ref, o_ref, h_ref):
    @pl.when(pl.program_id(1) == 0)
    def _():
        _norm_into(x_ref, g_ref, h_ref)

    o_ref[...] = _dot(h_ref[...], w_ref[...])


def _proj_hgrn_gate_kernel(x_ref, g_ref, w_ref, lb_ref, logf_ref, k_ref, h_ref):
    @pl.when(pl.program_id(1) == 0)
    def _():
        _norm_into(x_ref, g_ref, h_ref)

    z = _dot(h_ref[...], w_ref[...])
    lb = lb_ref[...]
    a = jnp.log1p(-lb) + _log_sigmoid(z)
    b = jnp.log(lb)
    logf_ref[...] = jnp.maximum(a, b) + jnp.log1p(jnp.exp(-jnp.abs(a - b)))
    k_ref[...] = (1.0 - lb) * jax.nn.sigmoid(-z)


def _proj_fox_gate_kernel(x_ref, g_ref, w_ref, bias_ref, logf_ref, cum_ref, h_ref, carry_ref,
                          *, seq, tm):
    i = pl.program_id(0)
    _norm_into(x_ref, g_ref, h_ref)
    logf = _log_sigmoid(_dot(h_ref[...], w_ref[...]) + bias_ref[...])
    logf_ref[...] = logf
    t = lax.broadcasted_iota(jnp.int32, (tm, tm), 0)
    s = lax.broadcasted_iota(jnp.int32, (tm, tm), 1)
    if seq >= tm:
        tri = _ones_where(s <= t)

        @pl.when(i % (seq // tm) == 0)
        def _():
            carry_ref[...] = jnp.zeros_like(carry_ref)

        cum = _dot3_rhs(tri, logf) + carry_ref[...]
        carry_ref[...] = cum[tm - 1:tm, :]
    else:
        shift = seq.bit_length() - 1
        same = lax.shift_right_logical(t, shift) == lax.shift_right_logical(s, shift)
        cum = _dot3_rhs(_ones_where(same & (s <= t)), logf)
    cum_ref[...] = cum


def _proj_specs(tm, D, tn):
    return [pl.BlockSpec((tm, D), lambda i, j: (i, 0)),
            pl.BlockSpec((1, D), lambda i, j: (0, 0)),
            pl.BlockSpec((D, tn), lambda i, j: (0, j))]


def _proj_heads(x, gain, w, hgain, flag):
    T, D = x.shape
    N = w.shape[1]
    tm, tn = _divisor_tile(T, ROW_TILE), _divisor_tile(N, COL_TILE)
    row = pl.BlockSpec((1, tn), lambda i, j: (0, j))
    return pl.pallas_call(
        _proj_heads_kernel,
        grid=(T // tm, N // tn),
        in_specs=_proj_specs(tm, D, tn) + [row, row],
        out_specs=pl.BlockSpec((tm, tn), lambda i, j: (i, j)),
        out_shape=jax.ShapeDtypeStruct((T, N), F32),
        scratch_shapes=[pltpu.VMEM((tm, D), BF16)],
        compiler_params=_cparams("parallel", "arbitrary"),
        name="proj_heads",
    )(x, gain, w, hgain, flag)


def _proj_plain(x, gain, w):
    T, D = x.shape
    N = w.shape[1]
    tm, tn = _divisor_tile(T, ROW_TILE), _divisor_tile(N, COL_TILE)
    return pl.pallas_call(
        _proj_plain_kernel,
        grid=(T // tm, N // tn),
        in_specs=_proj_specs(tm, D, tn),
        out_specs=pl.BlockSpec((tm, tn), lambda i, j: (i, j)),
        out_shape=jax.ShapeDtypeStruct((T, N), F32),
        scratch_shapes=[pltpu.VMEM((tm, D), BF16)],
        compiler_params=_cparams("parallel", "arbitrary"),
        name="proj_plain",
    )(x, gain, w)


def _proj_hgrn_gate(x, gain, w, lb):
    T, D = x.shape
    N = w.shape[1]
    tm, tn = _divisor_tile(T, ROW_TILE), _divisor_tile(N, COL_TILE)
    out = pl.BlockSpec((tm, tn), lambda i, j: (i, j))
    return pl.pallas_call(
        _proj_hgrn_gate_kernel,
        grid=(T // tm, N // tn),
        in_specs=_proj_specs(tm, D, tn) + [pl.BlockSpec((1, tn), lambda i, j: (0, j))],
        out_specs=[out, out],
        out_shape=[jax.ShapeDtypeStruct((T, N), F32)] * 2,
        scratch_shapes=[pltpu.VMEM((tm, D), BF16)],
        compiler_params=_cparams("parallel", "arbitrary"),
        name="proj_hgrn_gate",
    )(x, gain, w, lb)


def _proj_fox_gate(x, gain, w, bias, seq):
    T, D = x.shape
    tm = _divisor_tile(T, ROW_TILE)
    assert (seq % tm == 0) or (tm % seq == 0 and seq & (seq - 1) == 0)
    out = pl.BlockSpec((tm, LANES), lambda i: (i, 0))
    return pl.pallas_call(
        functools.partial(_proj_fox_gate_kernel, seq=seq, tm=tm),
        grid=(T // tm,),
        in_specs=[pl.BlockSpec((tm, D), lambda i: (i, 0)),
                  pl.BlockSpec((1, D), lambda i: (0, 0)),
                  pl.BlockSpec((D, LANES), lambda i: (0, 0)),
                  pl.BlockSpec((1, LANES), lambda i: (0, 0))],
        out_specs=[out, out],
        out_shape=[jax.ShapeDtypeStruct((T, LANES), F32)] * 2,
        scratch_shapes=[pltpu.VMEM((tm, D), BF16), pltpu.VMEM((1, LANES), F32)],
        compiler_params=_cparams("arbitrary"),
        name="proj_fox_gate",
    )(x, gain, w, bias)


def _fox_prompt_kernel(q_ref, kv_ref, ccol_ref, crow_ref, og_ref, o_ref, *, blk):
    h = pl.program_id(1)
    i = pl.program_id(2)
    q = q_ref[...].astype(BF16)
    ccol = _pick_lane(ccol_ref[...], h)

    def block(j, carry, diagonal):
        m, l, acc = carry
        off = pl.multiple_of(j * blk, blk)
        k = kv_ref[pl.ds(off, blk), 0:HEAD_DIM].astype(BF16)
        v = kv_ref[pl.ds(off, blk), HEAD_DIM:2 * HEAD_DIM].astype(BF16)
        s = _dot_nt(q, k) + ccol - _pick_row(crow_ref[:, pl.ds(off, blk)], h)
        if diagonal:
            t_io = lax.broadcasted_iota(jnp.int32, s.shape, 0)
            s_io = lax.broadcasted_iota(jnp.int32, s.shape, 1)
            s = jnp.where(s_io <= t_io, s, -jnp.inf)
        m_new = jnp.maximum(m, jnp.max(s, axis=1, keepdims=True))
        alpha = jnp.exp(m - m_new)
        p = jnp.exp(s - m_new)
        l = alpha * l + jnp.sum(p, axis=1, keepdims=True)
        acc = alpha * acc + _dot(p.astype(BF16), v)
        return m_new, l, acc

    init = (jnp.full((blk, 1), -jnp.inf, F32), jnp.zeros((blk, 1), F32),
            jnp.zeros((blk, HEAD_DIM), F32))
    carry = lax.fori_loop(0, i, lambda j, c: block(j, c, False), init)
    _, l, acc = block(i, carry, True)
    o_ref[...] = _rms_rows(acc / l, og_ref[...]).astype(BF16)


def _sb_prompt_kernel(q_ref, kv_ref, og_ref, o_ref, *, blk):
    i = pl.program_id(2)
    q = q_ref[...].astype(BF16)
    j_io = lax.broadcasted_iota(jnp.int32, (blk, blk), 0)
    s_io = lax.broadcasted_iota(jnp.int32, (blk, blk), 1)
    later = _ones_where(j_io > s_io)
    strict = s_io < j_io

    def block(j, carry, diagonal):
        run, acc = carry
        off = pl.multiple_of(j * blk, blk)
        k = kv_ref[pl.ds(off, blk), 0:HEAD_DIM].astype(BF16)
        v = kv_ref[pl.ds(off, blk), HEAD_DIM:2 * HEAD_DIM].astype(BF16)
        z = _dot_nt(q, k)
        lsm = _log_sigmoid(-z)
        if diagonal:
            lsm = jnp.where(strict, lsm, 0.0)
        a = jnp.exp(lsm + z + _dot3_lhs(lsm, later) + run)
        if diagonal:
            a = jnp.where(strict, a, 0.0)
        acc = acc + _dot(a.astype(BF16), v)
        run = run + jnp.sum(lsm, axis=1, keepdims=True)
        return run, acc

    carry = block(i, (jnp.zeros((blk, 1), F32), jnp.zeros((blk, HEAD_DIM), F32)), True)
    _, acc = lax.fori_loop(0, i, lambda jj, c: block(i - 1 - jj, c, False), carry)
    o_ref[...] = _rms_rows(acc, og_ref[...]).astype(BF16)


def _fox_prompt(a3, cum3, cum_t, out_gain3, n_heads, q_col0, kv_col0):
    B, S, _ = a3.shape
    blk = _divisor_tile(S, FOX_BLOCK)
    return pl.pallas_call(
        functools.partial(_fox_prompt_kernel, blk=blk),
        grid=(B, n_heads, S // blk),
        in_specs=[pl.BlockSpec((None, blk, HEAD_DIM), lambda b, h, i: (b, i, q_col0 + h)),
                  pl.BlockSpec((None, S, 2 * HEAD_DIM), lambda b, h, i: (b, 0, kv_col0 + h)),
                  pl.BlockSpec((None, blk, LANES), lambda b, h, i: (b, i, 0)),
                  pl.BlockSpec((None, n_heads, S), lambda b, h, i: (b, 0, 0)),
                  pl.BlockSpec((None, 1, HEAD_DIM), lambda b, h, i: (h, 0, 0))],
        out_specs=pl.BlockSpec((None, blk, HEAD_DIM), lambda b, h, i: (b, i, h)),
        out_shape=jax.ShapeDtypeStruct((B, S, n_heads * HEAD_DIM), BF16),
        compiler_params=_cparams("parallel", "parallel", "arbitrary"),
        name="fox_prompt",
    )(a3, a3, cum3, cum_t, out_gain3)


def _sb_prompt(a3, out_gain3, n_heads, q_col0, kv_col0, gain_row0):
    B, S, _ = a3.shape
    blk = _divisor_tile(S, SB_BLOCK)
    return pl.pallas_call(
        functools.partial(_sb_prompt_kernel, blk=blk),
        grid=(B, n_heads, S // blk),
        in_specs=[pl.BlockSpec((None, blk, HEAD_DIM), lambda b, h, i: (b, i, q_col0 + h)),
                  pl.BlockSpec((None, S, 2 * HEAD_DIM), lambda b, h, i: (b, 0, kv_col0 + h)),
                  pl.BlockSpec((None, 1, HEAD_DIM), lambda b, h, i: (gain_row0 + h, 0, 0))],
        out_specs=pl.BlockSpec((None, blk, HEAD_DIM), lambda b, h, i: (b, i, h)),
        out_shape=jax.ShapeDtypeStruct((B, S, n_heads * HEAD_DIM), BF16),
        compiler_params=_cparams("parallel", "parallel", "arbitrary"),
        name="sb_prompt",
    )(a3, a3, out_gain3)


def _hgrn_kernel(*refs, chunk, heads, has_state):
    if has_state:
        q_ref, g_ref, k_ref, i_ref, gate_ref, og_ref, s0_ref, o_ref, sout_ref, st_ref = refs
    else:
        q_ref, g_ref, k_ref, i_ref, gate_ref, og_ref, o_ref, sout_ref, st_ref = refs
    c = pl.program_id(2)
    width = heads * HEAD_DIM
    groups = chunk // SUBLANES

    @pl.when(c == 0)
    def _():
        for hh in range(heads):
            if has_state:
                st_ref[hh] = s0_ref[hh].T
            else:
                st_ref[hh] = jnp.zeros((HEAD_DIM, HEAD_DIM), F32)

    q = q_ref[...]
    g = g_ref[...]
    kk = k_ref[...]
    iv = i_ref[...].astype(BF16)

    rig = lax.broadcasted_iota(jnp.int32, (SUBLANES, width), 0)
    zero_row = jnp.zeros((1, width), F32)
    gg, bg, carry = [], [], zero_row
    for gi in range(groups):
        rows = g[gi * SUBLANES:(gi + 1) * SUBLANES, :]
        local = jnp.zeros((SUBLANES, width), F32)
        for j in range(SUBLANES):
            local = local + jnp.where(rig >= j, rows[j:j + 1, :], 0.0)
        b_rows = local + carry
        carry = b_rows[SUBLANES - 1:SUBLANES, :]
        gg.append(rows)
        bg.append(b_rows)
    b_last = carry

    def b_at(gi, j):
        return zero_row if gi < 0 else bg[gi][j:j + 1, :]

    def pick(values):
        m = SUBLANES // len(values)
        out = values[-1]
        for k in range(len(values) - 2, -1, -1):
            out = jnp.where(rig < (k + 1) * m, values[k], out)
        return out

    levels = []
    m = chunk // 2
    while m >= 1:
        levels.append(m)
        m //= 2

    q_lvl, k_lvl = {}, {}
    for m in levels:
        q_rows, k_rows = [], []
        for gi in range(groups):
            qg = q[gi * SUBLANES:(gi + 1) * SUBLANES, :]
            kg = kk[gi * SUBLANES:(gi + 1) * SUBLANES, :]
            if m == 1:
                u, v = gg[gi], None
            elif m >= SUBLANES:
                r = m // SUBLANES
                first = r * (gi // r)
                u = bg[gi] - b_at(first - 1, SUBLANES - 1)
                v = b_at(first + r - 1, SUBLANES - 1) - bg[gi]
            else:
                n = SUBLANES // m
                prev = pick([b_at(gi - 1, SUBLANES - 1)] + [b_at(gi, k * m - 1) for k in range(1, n)])
                last = pick([b_at(gi, k * m + m - 1) for k in range(n)])
                u = bg[gi] - prev
                v = last - bg[gi]
            q_rows.append(qg * jnp.exp(u))
            k_rows.append(kg if v is None else kg * jnp.exp(v))
        q_lvl[m] = jnp.concatenate(q_rows, axis=0).astype(BF16)
        k_lvl[m] = jnp.concatenate(k_rows, axis=0).astype(BF16)

    b_all = jnp.concatenate(bg, axis=0)
    q_state = (q * jnp.exp(b_all)).astype(BF16)
    k_state = (kk * jnp.exp(b_last - b_all)).astype(BF16)
    decay = jnp.exp(b_last)
    q_bf = q.astype(BF16)
    k_bf = kk.astype(BF16)

    t_io = lax.broadcasted_iota(jnp.int32, (chunk, chunk), 0)
    s_io = lax.broadcasted_iota(jnp.int32, (chunk, chunk), 1)
    masks = {}
    for m in levels:
        sh = (2 * m).bit_length() - 1
        same = lax.shift_right_logical(t_io, sh) == lax.shift_right_logical(s_io, sh)
        upper = jnp.bitwise_and(t_io, 2 * m - 1) >= m
        lower = jnp.bitwise_and(s_io, 2 * m - 1) < m
        masks[m] = same & upper & lower

    for hh in range(heads):
        sl = slice(hh * HEAD_DIM, (hh + 1) * HEAD_DIM)
        w = jnp.where(t_io == s_io, _dot_nt(q_bf[:, sl], k_bf[:, sl]), 0.0)
        for m in levels:
            w = w + jnp.where(masks[m], _dot_nt(q_lvl[m][:, sl], k_lvl[m][:, sl]), 0.0)
        st = st_ref[hh]
        o = _dot(w.astype(BF16), iv[:, sl]) + _dot_nt(q_state[:, sl], st.astype(BF16))
        st_new = st * decay[:, sl] + _dot_tn(iv[:, sl], k_state[:, sl])
        st_ref[hh] = st_new
        gate = gate_ref[:, sl]
        o_ref[:, sl] = (_rms_rows(o, og_ref[:, sl]) * (gate * jax.nn.sigmoid(gate))).astype(BF16)

    @pl.when(c == pl.num_programs(2) - 1)
    def _():
        for hh in range(heads):
            sout_ref[hh] = st_ref[hh].T


def _hgrn(bm3, logf3, k3, out_gain, n_heads, gain_col0, s0):
    B, S, W = logf3.shape
    chunk = _divisor_tile(S, HG_CHUNK)
    hb = _divisor_tile(n_heads, HG_HEADS_PER_STEP, align=1)
    wb = hb * HEAD_DIM
    ng = n_heads // hb
    assert gain_col0 % wb == 0

    def col(k):
        return pl.BlockSpec((None, chunk, wb), lambda b, h, c: (b, c, k * ng + h))

    in_specs = [col(0),
                pl.BlockSpec((None, chunk, wb), lambda b, h, c: (b, c, h)),
                pl.BlockSpec((None, chunk, wb), lambda b, h, c: (b, c, h)),
                col(1), col(2),
                pl.BlockSpec((1, wb), lambda b, h, c: (0, gain_col0 // wb + h))]
    args = [bm3, logf3, k3, bm3, bm3, out_gain]
    state_spec = pl.BlockSpec((None, hb, HEAD_DIM, HEAD_DIM), lambda b, h, c: (b, h, 0, 0))
    if s0 is not None:
        in_specs.append(state_spec)
        args.append(s0)
    return pl.pallas_call(
        functools.partial(_hgrn_kernel, chunk=chunk, heads=hb, has_state=s0 is not None),
        grid=(B, ng, S // chunk),
        in_specs=in_specs,
        out_specs=[pl.BlockSpec((None, chunk, wb), lambda b, h, c: (b, c, h)), state_spec],
        out_shape=[jax.ShapeDtypeStruct((B, S, W), BF16),
                   jax.ShapeDtypeStruct((B, n_heads, HEAD_DIM, HEAD_DIM), F32)],
        scratch_shapes=[pltpu.VMEM((hb, HEAD_DIM, HEAD_DIM), F32)],
        compiler_params=_cparams("parallel", "parallel", "arbitrary"),
        name="hgrn",
    )(*args)


def _place_queries(q_ref, qbd_ref, n_heads, n_new):
    qbd_ref[...] = jnp.zeros_like(qbd_ref)
    for h in range(n_heads):
        qbd_ref[h * n_new:(h + 1) * n_new, h * 2 * HEAD_DIM:h * 2 * HEAD_DIM + HEAD_DIM] = (
            q_ref[:, h * HEAD_DIM:(h + 1) * HEAD_DIM].astype(BF16))


def _pad_new_rows(kvnew_ref, page_rows):
    new = kvnew_ref[...]
    pad = jnp.zeros((page_rows - new.shape[0], new.shape[1]), F32)
    return jnp.concatenate([new, pad], axis=0).astype(BF16)


def _fox_sample_kernel(pt_ref, q_ref, kvnew_ref, e_ref, et_ref, page_ref, lft_ref, og_ref, o_ref,
                       qbd_ref, m_ref, l_ref, acc_ref, run_ref, ecol_ref, *, n_heads, n_new, page_rows):
    del pt_ref
    p = pl.program_id(1)
    rows = n_heads * n_new
    row_io = lax.broadcasted_iota(jnp.int32, (rows, page_rows), 0)
    col_io = lax.broadcasted_iota(jnp.int32, (rows, page_rows), 1)

    def update(kv, bias, mask):
        s = _dot_nt(qbd_ref[...], kv) + bias
        if mask is not None:
            s = jnp.where(mask, s, -jnp.inf)
        m_old = m_ref[...]
        m_new = jnp.maximum(m_old, jnp.max(s, axis=1, keepdims=True))
        alpha = jnp.exp(m_old - m_new)
        pm = jnp.exp(s - m_new)
        l_ref[...] = alpha * l_ref[...] + jnp.sum(pm, axis=1, keepdims=True)
        acc_ref[...] = alpha * acc_ref[...] + _dot(pm.astype(BF16), kv)
        m_ref[...] = m_new

    @pl.when(p == 0)
    def _():
        _place_queries(q_ref, qbd_ref, n_heads, n_new)
        m_ref[...] = jnp.full_like(m_ref, -jnp.inf)
        l_ref[...] = jnp.zeros_like(l_ref)
        acc_ref[...] = jnp.zeros_like(acc_ref)
        run_ref[...] = jnp.zeros_like(run_ref)
        e = e_ref[...]
        ecol = jnp.concatenate([_pick_lane(e, h) for h in range(n_heads)], axis=0)
        ecol_ref[...] = ecol
        t_of_row = jnp.bitwise_and(row_io, n_new - 1)
        update(_pad_new_rows(kvnew_ref, page_rows), ecol - _repeat_rows(et_ref[...], n_new),
               col_io <= t_of_row)

    @pl.when(p > 0)
    def _():
        j_io = lax.broadcasted_iota(jnp.int32, (page_rows, page_rows), 0)
        s_io = lax.broadcasted_iota(jnp.int32, (page_rows, page_rows), 1)
        lf = _repeat_rows(lft_ref[...], n_new)
        suffix = _dot3_lhs(lf, _ones_where(j_io > s_io)) + run_ref[...]
        update(page_ref[...].astype(BF16), ecol_ref[...] + suffix, None)
        run_ref[...] += jnp.sum(lf, axis=1, keepdims=True)

    @pl.when(p == pl.num_programs(1) - 1)
    def _():
        for h in range(n_heads):
            r = slice(h * n_new, (h + 1) * n_new)
            o = acc_ref[r, h * 2 * HEAD_DIM + HEAD_DIM:(h + 1) * 2 * HEAD_DIM] / l_ref[r, :]
            sl = slice(h * HEAD_DIM, (h + 1) * HEAD_DIM)
            o_ref[:, sl] = _rms_rows(o, og_ref[:, sl]).astype(BF16)


def _sb_sample_kernel(pt_ref, q_ref, kvnew_ref, page_ref, og_ref, o_ref,
                      qbd_ref, acc_ref, run_ref, *, n_heads, n_new, page_rows):
    del pt_ref
    p = pl.program_id(1)
    rows = n_heads * n_new
    row_io = lax.broadcasted_iota(jnp.int32, (rows, page_rows), 0)
    col_io = lax.broadcasted_iota(jnp.int32, (rows, page_rows), 1)

    def update(kv, mask):
        j_io = lax.broadcasted_iota(jnp.int32, (page_rows, page_rows), 0)
        s_io = lax.broadcasted_iota(jnp.int32, (page_rows, page_rows), 1)
        z = _dot_nt(qbd_ref[...], kv)
        lsm = _log_sigmoid(-z)
        if mask is not None:
            lsm = jnp.where(mask, lsm, 0.0)
        a = jnp.exp(lsm + z + _dot3_lhs(lsm, _ones_where(j_io > s_io)) + run_ref[...])
        if mask is not None:
            a = jnp.where(mask, a, 0.0)
        acc_ref[...] += _dot(a.astype(BF16), kv)
        run_ref[...] += jnp.sum(lsm, axis=1, keepdims=True)

    @pl.when(p == 0)
    def _():
        _place_queries(q_ref, qbd_ref, n_heads, n_new)
        acc_ref[...] = jnp.zeros_like(acc_ref)
        run_ref[...] = jnp.zeros_like(run_ref)
        t_of_row = jnp.bitwise_and(row_io, n_new - 1)
        update(_pad_new_rows(kvnew_ref, page_rows), col_io < t_of_row)

    @pl.when(p > 0)
    def _():
        update(page_ref[...].astype(BF16), None)

    @pl.when(p == pl.num_programs(1) - 1)
    def _():
        for h in range(n_heads):
            o = acc_ref[h * n_new:(h + 1) * n_new, h * 2 * HEAD_DIM + HEAD_DIM:(h + 1) * 2 * HEAD_DIM]
            sl = slice(h * HEAD_DIM, (h + 1) * HEAD_DIM)
            o_ref[:, sl] = _rms_rows(o, og_ref[:, sl]).astype(BF16)


def _page_index(n_pages, layer):
    def index(b, p, pt):
        return (pt[b * n_pages + n_pages - jnp.maximum(p, 1)], layer, 0, 0)
    return index


def _fox_sample(page_table, a3, cum3, cum_t, cache_kv, cache_lft, out_gain, layer, n_heads,
                q_blk, kv_blk):
    B, L, _ = a3.shape
    n_pages = page_table.shape[1]
    page_rows, kvw = cache_kv.shape[2], cache_kv.shape[3]
    assert L & (L - 1) == 0 and L % SUBLANES == 0 and L <= page_rows and page_rows == LANES
    rows = n_heads * L
    qw = n_heads * HEAD_DIM
    grid_spec = pltpu.PrefetchScalarGridSpec(
        num_scalar_prefetch=1,
        grid=(B, n_pages + 1),
        in_specs=[pl.BlockSpec((None, L, qw), lambda b, p, pt: (b, 0, q_blk)),
                  pl.BlockSpec((None, L, kvw), lambda b, p, pt: (b, 0, kv_blk)),
                  pl.BlockSpec((None, L, LANES), lambda b, p, pt: (b, 0, 0)),
                  pl.BlockSpec((None, n_heads, LANES), lambda b, p, pt: (b, 0, 0)),
                  pl.BlockSpec((None, None, page_rows, kvw), _page_index(n_pages, layer)),
                  pl.BlockSpec((None, None, n_heads, page_rows), _page_index(n_pages, layer)),
                  pl.BlockSpec((1, qw), lambda b, p, pt: (0, 0))],
        out_specs=pl.BlockSpec((None, L, qw), lambda b, p, pt: (b, 0, 0)),
        scratch_shapes=[pltpu.VMEM((rows, kvw), BF16), pltpu.VMEM((rows, 1), F32),
                        pltpu.VMEM((rows, 1), F32), pltpu.VMEM((rows, kvw), F32),
                        pltpu.VMEM((rows, 1), F32), pltpu.VMEM((rows, 1), F32)])
    return pl.pallas_call(
        functools.partial(_fox_sample_kernel, n_heads=n_heads, n_new=L, page_rows=page_rows),
        grid_spec=grid_spec,
        out_shape=jax.ShapeDtypeStruct((B, L, qw), BF16),
        compiler_params=_cparams("parallel", "arbitrary"),
        name="fox_sample",
    )(page_table.reshape(-1), a3, a3, cum3, cum_t, cache_kv, cache_lft, out_gain)


def _sb_sample(page_table, a3, cache_kv, out_gain, layer, n_heads, q_blk, kv_blk, gain_blk):
    B, L, _ = a3.shape
    n_pages = page_table.shape[1]
    page_rows, kvw = cache_kv.shape[2], cache_kv.shape[3]
    assert L & (L - 1) == 0 and L % SUBLANES == 0 and L <= page_rows and page_rows == LANES
    rows = n_heads * L
    qw = n_heads * HEAD_DIM
    grid_spec = pltpu.PrefetchScalarGridSpec(
        num_scalar_prefetch=1,
        grid=(B, n_pages + 1),
        in_specs=[pl.BlockSpec((None, L, qw), lambda b, p, pt: (b, 0, q_blk)),
                  pl.BlockSpec((None, L, kvw), lambda b, p, pt: (b, 0, kv_blk)),
                  pl.BlockSpec((None, None, page_rows, kvw), _page_index(n_pages, layer)),
                  pl.BlockSpec((1, qw), lambda b, p, pt: (0, gain_blk))],
        out_specs=pl.BlockSpec((None, L, qw), lambda b, p, pt: (b, 0, 0)),
        scratch_shapes=[pltpu.VMEM((rows, kvw), BF16), pltpu.VMEM((rows, kvw), F32),
                        pltpu.VMEM((rows, 1), F32)])
    return pl.pallas_call(
        functools.partial(_sb_sample_kernel, n_heads=n_heads, n_new=L, page_rows=page_rows),
        grid_spec=grid_spec,
        out_shape=jax.ShapeDtypeStruct((B, L, qw), BF16),
        compiler_params=_cparams("parallel", "arbitrary"),
        name="sb_sample",
    )(page_table.reshape(-1), a3, a3, cache_kv, out_gain)


def _out_proj_kernel(x_ref, a_ref, b_ref, c_ref, wa_ref, wb_ref, wc_ref, o_ref):
    o_ref[...] = (x_ref[...] + _dot(a_ref[...], wa_ref[...]) + _dot(b_ref[...], wb_ref[...])
                  + _dot(c_ref[...], wc_ref[...]))


def _out_proj(x, oa, ob, oc, wa, wb, wc):
    T, D = x.shape
    tm, tn = _divisor_tile(T, ROW_TILE), _divisor_tile(D, COL_TILE)

    def act(a):
        return pl.BlockSpec((tm, a.shape[1]), lambda i, j: (i, 0))

    def wgt(w):
        return pl.BlockSpec((w.shape[0], tn), lambda i, j: (0, j))

    return pl.pallas_call(
        _out_proj_kernel,
        grid=(T // tm, D // tn),
        in_specs=[pl.BlockSpec((tm, tn), lambda i, j: (i, j)), act(oa), act(ob), act(oc),
                  wgt(wa), wgt(wb), wgt(wc)],
        out_specs=pl.BlockSpec((tm, tn), lambda i, j: (i, j)),
        out_shape=jax.ShapeDtypeStruct((T, D), F32),
        compiler_params=_cparams("parallel", "arbitrary"),
        name="out_proj",
    )(x, oa, ob, oc, wa, wb, wc)


def _ple_kernel(x_ref, g_ref, wg_ref, p_ref, wp_ref, o_ref, h_ref, *, tn):
    j = pl.program_id(1)

    @pl.when(j == 0)
    def _():
        _norm_into(x_ref, g_ref, h_ref)

    gate = jax.nn.sigmoid(_dot(h_ref[...], wg_ref[...]))
    emb = _dot(p_ref[...].astype(BF16), wp_ref[...])
    o_ref[...] = x_ref[:, pl.ds(pl.multiple_of(j * tn, tn), tn)] + gate * emb


def _ple(x, gain, wg, p, wp):
    T, D = x.shape
    tm, tn = _divisor_tile(T, ROW_TILE), _divisor_tile(D, COL_TILE)
    return pl.pallas_call(
        functools.partial(_ple_kernel, tn=tn),
        grid=(T // tm, D // tn),
        in_specs=[pl.BlockSpec((tm, D), lambda i, j: (i, 0)),
                  pl.BlockSpec((1, D), lambda i, j: (0, 0)),
                  pl.BlockSpec((D, tn), lambda i, j: (0, j)),
                  pl.BlockSpec((tm, p.shape[1]), lambda i, j: (i, 0)),
                  pl.BlockSpec((p.shape[1], tn), lambda i, j: (0, j))],
        out_specs=pl.BlockSpec((tm, tn), lambda i, j: (i, j)),
        out_shape=jax.ShapeDtypeStruct((T, D), F32),
        scratch_shapes=[pltpu.VMEM((tm, D), BF16)],
        compiler_params=_cparams("parallel", "arbitrary"),
        name="ple",
    )(x, gain, wg, p, wp)


def _interleave_heads(a, b, n_heads):
    rows = a.shape[0]
    return jnp.stack([a.reshape(rows, n_heads, HEAD_DIM), b.reshape(rows, n_heads, HEAD_DIM)],
                     axis=2).reshape(rows, n_heads * 2 * HEAD_DIM)


def _layer_weights(l, hf, hs, hg, lb, ffn1_w1, ffn1_w3, ffn1_w2, w_in, fox_f_bias, fox_q_norm,
                   fox_k_norm, sb_q_norm, sb_k_norm, w_out, ffn2_w1, ffn2_w3, ffn2_w2, w_ple_gate,
                   w_ple_proj):
    fw, sw, gw = hf * HEAD_DIM, hs * HEAD_DIM, hg * HEAD_DIM
    offs, o = [], 0
    for size in (fw, fw, fw, hf, sw, sw, sw, gw, gw, gw, gw):
        offs.append((o, o + size))
        o += size
    w = w_in[l]
    seg = [w[:, a:b] for a, b in offs]
    fq, fk, fv, ff, sq, sk, sv, hq, hfg, hi, hgt = seg
    scale = HEAD_DIM ** -0.5
    ones, zeros = jnp.ones((HEAD_DIM,), F32), jnp.zeros((HEAD_DIM,), F32)
    w_heads = jnp.concatenate([_interleave_heads(fk, fv, hf), _interleave_heads(sk, sv, hs), fq, sq],
                              axis=1).astype(BF16)
    gain_heads = jnp.concatenate([jnp.tile(jnp.concatenate([fox_k_norm[l], ones]), hf),
                                  jnp.tile(jnp.concatenate([sb_k_norm[l], ones]), hs),
                                  jnp.tile(fox_q_norm[l] * scale, hf),
                                  jnp.tile(sb_q_norm[l] * scale, hs)])[None, :]
    flag_heads = jnp.concatenate([jnp.tile(jnp.concatenate([ones, zeros]), hf + hs),
                                  jnp.ones((fw + sw,), F32)])[None, :]
    return dict(
        ffn1=(ffn1_w1[l].astype(BF16), ffn1_w3[l].astype(BF16), ffn1_w2[l].astype(BF16)),
        ffn2=(ffn2_w1[l].astype(BF16), ffn2_w3[l].astype(BF16), ffn2_w2[l].astype(BF16)),
        w_heads=w_heads, gain_heads=gain_heads, flag_heads=flag_heads,
        w_plain=jnp.concatenate([hq, hi, hgt], axis=1).astype(BF16),
        w_hgate=hfg.astype(BF16), lb=lb[None, :],
        w_fgate=jnp.pad(ff, ((0, 0), (0, LANES - hf))).astype(BF16),
        b_fgate=jnp.pad(fox_f_bias[l], (0, LANES - hf))[None, :],
        w_out=(w_out[l][:fw].astype(BF16), w_out[l][fw:fw + sw].astype(BF16),
               w_out[l][fw + sw:].astype(BF16)),
        w_ple_gate=w_ple_gate[l].astype(BF16), w_ple_proj=w_ple_proj[l].astype(BF16))


def kernel(x_prompt, x_sample, p_prompt, p_sample, cache_fox_kv, cache_fox_logf, cache_sb_kv, state_hgrn,
           page_table, ffn1_norm, ffn1_w1, ffn1_w3, ffn1_w2, mix_norm, w_in, fox_f_bias, fox_q_norm,
           fox_k_norm, sb_q_norm, sb_k_norm, hg_lower_bound, out_norm, w_out, ffn2_norm, ffn2_w1, ffn2_w3,
           ffn2_w2, ple_norm, w_ple_gate, w_ple_proj):
    depth = w_in.shape[0]
    B, S, D = x_prompt.shape
    Bs, L, _ = x_sample.shape
    hf, hs = fox_f_bias.shape[1], cache_sb_kv.shape[3]
    hg = state_hgrn.shape[2]
    assert hf == hs and hf % 2 == 0
    fw, sw, gw = hf * HEAD_DIM, hs * HEAD_DIM, hg * HEAD_DIM
    n_pool, _, page_rows, _, _ = cache_fox_kv.shape

    gamma = jnp.cumsum(jax.nn.softmax(hg_lower_bound.astype(F32), axis=0), axis=0)
    cache_fkv = cache_fox_kv.reshape(n_pool, depth, page_rows, hf * 2 * HEAD_DIM)
    cache_skv = cache_sb_kv.reshape(n_pool, depth, page_rows, hs * 2 * HEAD_DIM)
    cache_lft = jnp.swapaxes(cache_fox_logf.astype(F32), 2, 3)

    fq_head0 = 2 * (hf + hs)
    sq_head0 = fq_head0 + hf

    xp = x_prompt.reshape(B * S, D)
    xs = x_sample.reshape(Bs * L, D)
    hg_pad = _divisor_tile(max(L, HG_CHUNK), HG_CHUNK)
    outs = {k: [] for k in ("pf_kv", "pf_logf", "psb_kv", "phg", "sf_kv", "sf_logf", "ssb_kv", "shg")}

    for l in range(depth):
        lb = gamma[l - 1] if l > 0 else jnp.zeros_like(gamma[0])
        wl = _layer_weights(l, hf, hs, hg, lb, ffn1_w1, ffn1_w3, ffn1_w2, w_in, fox_f_bias, fox_q_norm,
                            fox_k_norm, sb_q_norm, sb_k_norm, w_out, ffn2_w1, ffn2_w3, ffn2_w2,
                            w_ple_gate, w_ple_proj)
        out_gain = out_norm[l][None, :]
        out_gain3 = out_norm[l].reshape(-1, 1, HEAD_DIM)

        def pre_mix(x, seq):
            x = _ffn(x, ffn1_norm[l][None, :], *wl["ffn1"])
            mg = mix_norm[l][None, :]
            a = _proj_heads(x, mg, wl["w_heads"], wl["gain_heads"], wl["flag_heads"])
            bm = _proj_plain(x, mg, wl["w_plain"])
            glogf, gk = _proj_hgrn_gate(x, mg, wl["w_hgate"], wl["lb"])
            flogf, fcum = _proj_fox_gate(x, mg, wl["w_fgate"], wl["b_fgate"], seq)
            return x, a, bm, glogf, gk, flogf, fcum

        def post_mix(x, o_fox, o_sb, o_hg, p_l):
            x = _out_proj(x, o_fox, o_sb, o_hg, *wl["w_out"])
            x = _ffn(x, ffn2_norm[l][None, :], *wl["ffn2"])
            return _ple(x, ple_norm[l][None, :], wl["w_ple_gate"], p_l, wl["w_ple_proj"])

        xp, a, bm, glogf, gk, flogf, fcum = pre_mix(xp, S)
        a3 = a.reshape(B, S, -1)
        cum_t = jnp.swapaxes(fcum.reshape(B, S, LANES)[:, :, :hf], 1, 2)
        o_fox = _fox_prompt(a3, fcum.reshape(B, S, LANES), cum_t, out_gain3, hf, fq_head0, 0)
        o_sb = _sb_prompt(a3, out_gain3, hs, sq_head0, hf, hf)
        o_hg, st = _hgrn(bm.reshape(B, S, -1), glogf.reshape(B, S, gw), gk.reshape(B, S, gw),
                         out_gain, hg, fw + sw, None)
        xp = post_mix(xp, o_fox.reshape(B * S, fw), o_sb.reshape(B * S, sw), o_hg.reshape(B * S, gw),
                      p_prompt[l].reshape(B * S, -1))
        outs["pf_kv"].append(a3[:, :, :2 * fw].reshape(B, S, hf, 2 * HEAD_DIM))
        outs["pf_logf"].append(flogf.reshape(B, S, LANES)[:, :, :hf])
        outs["psb_kv"].append(a3[:, :, 2 * fw:2 * (fw + sw)].reshape(B, S, hs, 2 * HEAD_DIM))
        outs["phg"].append(st)

        xs, a, bm, glogf, gk, flogf, fcum = pre_mix(xs, L)
        a3 = a.reshape(Bs, L, -1)
        cum3 = fcum.reshape(Bs, L, LANES)
        cum_t = jnp.pad(jnp.swapaxes(cum3[:, :, :hf], 1, 2), ((0, 0), (0, 0), (0, LANES - L)))
        o_fox = _fox_sample(page_table, a3, cum3, cum_t, cache_fkv, cache_lft, out_gain, l, hf, 4, 0)
        o_sb = _sb_sample(page_table, a3, cache_skv, out_gain, l, hs, 5, 1, 1)

        def pad_tokens(z):
            return jnp.pad(z.reshape(Bs, L, -1), ((0, 0), (0, hg_pad - L), (0, 0)))

        o_hg, st = _hgrn(pad_tokens(bm), pad_tokens(glogf), pad_tokens(gk), out_gain, hg, fw + sw,
                         state_hgrn[l].astype(F32))
        xs = post_mix(xs, o_fox.reshape(Bs * L, fw), o_sb.reshape(Bs * L, sw),
                      o_hg[:, :L].reshape(Bs * L, gw), p_sample[l].reshape(Bs * L, -1))
        outs["sf_kv"].append(a3[:, :, :2 * fw].reshape(Bs, L, hf, 2 * HEAD_DIM))
        outs["sf_logf"].append(flogf.reshape(Bs, L, LANES)[:, :, :hf])
        outs["ssb_kv"].append(a3[:, :, 2 * fw:2 * (fw + sw)].reshape(Bs, L, hs, 2 * HEAD_DIM))
        outs["shg"].append(st)

    return (xp.reshape(B, S, D), xs.reshape(Bs, L, D),
            jnp.stack(outs["pf_kv"], axis=1), jnp.stack(outs["pf_logf"], axis=1),
            jnp.stack(outs["psb_kv"], axis=1), jnp.stack(outs["phg"], axis=0),
            jnp.stack(outs["sf_kv"], axis=1), jnp.stack(outs["sf_logf"], axis=1),
            jnp.stack(outs["ssb_kv"], axis=1), jnp.stack(outs["shg"], axis=0))
```

```python
import functools

import jax
import jax.numpy as jnp
from jax import lax
from jax.experimental import pallas as pl
from jax.experimental.pallas import tpu as pltpu

F32 = jnp.float32
BF16 = jnp.bfloat16

HEAD_DIM = 128
EPS = 1e-6
LANES = 128
SUBLANES = 8
BF16_ROWS = 16
VMEM_LIMIT_BYTES = 56 << 20
VMEM_LIMIT_FFN_BYTES = 60 << 20
ROW_TILE = 512
COL_TILE = 1024
FF_TILE = 256
ATTN_BLOCK = 256
ATTN_HEADS_PER_STEP = 2
HG_CHUNK = 128
HG_HEADS_PER_STEP = 4
PAGES_PER_STEP = 4
EXP_UNDERFLOW = -104.0


def _cparams(*sem, vmem=VMEM_LIMIT_BYTES):
    return pltpu.CompilerParams(dimension_semantics=sem, vmem_limit_bytes=vmem)


def _divisor_tile(n, pref, align=LANES):
    if n <= pref:
        return n
    for t in range(pref - pref % align, 0, -align):
        if n % t == 0:
            return t
    raise ValueError((n, pref, align))


def _dot(a, b):
    return jnp.dot(a, b, preferred_element_type=F32)


def _dot_nt(a, b):
    return lax.dot_general(a, b, (((1,), (1,)), ((), ())), preferred_element_type=F32)


def _dot_tn(a, b):
    return lax.dot_general(a, b, (((0,), (0,)), ((), ())), preferred_element_type=F32)


def _dot3_rhs(m_bf16, x):
    hi = x.astype(BF16)
    r1 = x - hi.astype(F32)
    mid = r1.astype(BF16)
    lo = (r1 - mid.astype(F32)).astype(BF16)
    return _dot(m_bf16, hi) + _dot(m_bf16, mid) + _dot(m_bf16, lo)


def _dot2_lhs(x, m_bf16):
    hi = x.astype(BF16)
    lo = (x - hi.astype(F32)).astype(BF16)
    return _dot(hi, m_bf16) + _dot(lo, m_bf16)


def _log_sigmoid(x):
    return jnp.minimum(x, 0.0) - jnp.log1p(jnp.exp(-jnp.abs(x)))


def _rms_rows(x, gain):
    return x * lax.rsqrt(jnp.mean(x * x, axis=-1, keepdims=True) + EPS) * gain


def _ones_where(cond):
    return jnp.where(cond, 1.0, 0.0).astype(BF16)


def _later_matrix(n):
    j_io = lax.broadcasted_iota(jnp.int32, (n, n), 0)
    s_io = lax.broadcasted_iota(jnp.int32, (n, n), 1)
    return _ones_where(j_io > s_io)


def _pick_lane(x, idx):
    lane = lax.broadcasted_iota(jnp.int32, x.shape, 1)
    return jnp.sum(jnp.where(lane == idx, x, 0.0), axis=1, keepdims=True)


def _pick_row(x, idx):
    row = lax.broadcasted_iota(jnp.int32, x.shape, 0)
    return jnp.sum(jnp.where(row == idx, x, 0.0), axis=0, keepdims=True)


def _repeat_rows(x, reps):
    return jnp.concatenate(
        [jnp.broadcast_to(x[h:h + 1, :], (reps, x.shape[1])) for h in range(x.shape[0])], axis=0)


def _ffn_kernel(*refs, emit_norm):
    if emit_norm:
        x_ref, g_ref, w1_ref, w3_ref, w2_ref, g2_ref, o_ref, h2_ref, h_ref = refs
    else:
        x_ref, g_ref, w1_ref, w3_ref, w2_ref, o_ref, h_ref = refs

    @pl.when(pl.program_id(1) == 0)
    def _():
        x = x_ref[...]
        h_ref[...] = _rms_rows(x, g_ref[...]).astype(BF16)
        o_ref[...] = x

    h = h_ref[...]
    a = _dot(h, w1_ref[...])
    b = _dot(h, w3_ref[...])
    gated = (a * jax.nn.sigmoid(a) * b * 0.5).astype(BF16)
    o_ref[...] += _dot(gated, w2_ref[...])

    if emit_norm:
        @pl.when(pl.program_id(1) == pl.num_programs(1) - 1)
        def _():
            h2_ref[...] = _rms_rows(o_ref[...], g2_ref[...]).astype(BF16)


def _ffn(x, gain, w1, w3, w2, layer, next_gain=None):
    T, D = x.shape
    F = w1.shape[2]
    tm = _divisor_tile(T, ROW_TILE)
    tf = _divisor_tile(F, FF_TILE)
    emit = next_gain is not None
    row = pl.BlockSpec((1, D), lambda i, j: (0, 0))
    tile = pl.BlockSpec((tm, D), lambda i, j: (i, 0))
    in_specs = [pl.BlockSpec((tm, D), lambda i, j: (i, 0), pipeline_mode=pl.Buffered(1)), row,
                pl.BlockSpec((None, D, tf), lambda i, j: (layer, 0, j)),
                pl.BlockSpec((None, D, tf), lambda i, j: (layer, 0, j)),
                pl.BlockSpec((None, tf, D), lambda i, j: (layer, j, 0))]
    args = [x, gain, w1, w3, w2]
    out_specs, out_shape = tile, jax.ShapeDtypeStruct((T, D), F32)
    if emit:
        in_specs.append(row)
        args.append(next_gain)
        out_specs = [tile, tile]
        out_shape = [out_shape, jax.ShapeDtypeStruct((T, D), BF16)]
    return pl.pallas_call(
        functools.partial(_ffn_kernel, emit_norm=emit),
        grid=(T // tm, F // tf),
        in_specs=in_specs, out_specs=out_specs, out_shape=out_shape,
        scratch_shapes=[pltpu.VMEM((tm, D), BF16)],
        compiler_params=_cparams("parallel", "arbitrary", vmem=VMEM_LIMIT_FFN_BYTES),
        name="ffn",
    )(*args)


def _proj_q_kernel(h_ref, w_ref, hgain_ref, o_ref):
    z = _dot(h_ref[...], w_ref[...])
    for c in range(z.shape[1] // HEAD_DIM):
        sl = slice(c * HEAD_DIM, (c + 1) * HEAD_DIM)
        o_ref[:, sl] = _rms_rows(z[:, sl], hgain_ref[:, sl])


def _proj_kv_kernel(*refs):
    h_ref, wk_ref, wv_ref, kgain_ref = refs[:4]
    o_ref = refs[-1]
    h = h_ref[...]
    zk = _dot(h, wk_ref[...])
    zv = _dot(h, wv_ref[...])
    lead = o_ref.shape[:-1]
    for c in range(zk.shape[1] // HEAD_DIM):
        sl = slice(c * HEAD_DIM, (c + 1) * HEAD_DIM)
        k_out = slice(2 * c * HEAD_DIM, (2 * c + 1) * HEAD_DIM)
        v_out = slice((2 * c + 1) * HEAD_DIM, (2 * c + 2) * HEAD_DIM)
        o_ref[..., k_out] = _rms_rows(zk[:, sl], kgain_ref[...]).reshape(*lead, HEAD_DIM)
        o_ref[..., v_out] = zv[:, sl].reshape(*lead, HEAD_DIM)


def _proj_plain_kernel(h_ref, w_ref, o_ref):
    o_ref[...] = _dot(h_ref[...], w_ref[...])


def _proj_hgrn_gate_kernel(h_ref, w_ref, lb_ref, logf_ref, k_ref):
    z = _dot(h_ref[...], w_ref[...])
    lb = lb_ref[...]
    a = jnp.log1p(-lb) + _log_sigmoid(z)
    b = jnp.log(lb)
    logf_ref[...] = jnp.maximum(a, b) + jnp.log1p(jnp.exp(-jnp.abs(a - b)))
    k_ref[...] = (1.0 - lb) * jax.nn.sigmoid(-z)


def _proj_fox_gate_kernel(h_ref, w_ref, bias_ref, logf_ref, cum_ref, carry_ref, *, seq, tm):
    i = pl.program_id(0)
    logf = _log_sigmoid(_dot(h_ref[...], w_ref[...]) + bias_ref[...])
    logf_ref[...] = logf
    t = lax.broadcasted_iota(jnp.int32, (tm, tm), 0)
    s = lax.broadcasted_iota(jnp.int32, (tm, tm), 1)
    if seq >= tm:
        @pl.when(i % (seq // tm) == 0)
        def _():
            carry_ref[...] = jnp.zeros_like(carry_ref)

        cum = _dot3_rhs(_ones_where(s <= t), logf) + carry_ref[...]
        carry_ref[...] = cum[tm - 1:tm, :]
    else:
        shift = seq.bit_length() - 1
        same = lax.shift_right_logical(t, shift) == lax.shift_right_logical(s, shift)
        cum = _dot3_rhs(_ones_where(same & (s <= t)), logf)
    cum_ref[...] = cum


def _proj_in_specs(tm, D, tn, layer, seg_cols, seg_width):
    per = seg_width // tn
    starts = [c // tn for c in seg_cols]
    assert seg_width % tn == 0 and all(c % tn == 0 for c in seg_cols)

    def w_index(i, j):
        blk = starts[-1]
        for k in range(len(starts) - 2, -1, -1):
            blk = jnp.where(j // per == k, starts[k], blk)
        return (layer, 0, blk + j % per)

    return [pl.BlockSpec((tm, D), lambda i, j: (i, 0)), pl.BlockSpec((None, D, tn), w_index)]


def _proj_tiles(h, seg_cols, seg_width):
    T, D = h.shape
    N = len(seg_cols) * seg_width
    return T, D, N, _divisor_tile(T, ROW_TILE), _divisor_tile(seg_width, COL_TILE)


def _proj_q(h, w, layer, seg_cols, seg_width, hgain):
    T, D, N, tm, tn = _proj_tiles(h, seg_cols, seg_width)
    return pl.pallas_call(
        _proj_q_kernel,
        grid=(T // tm, N // tn),
        in_specs=_proj_in_specs(tm, D, tn, layer, seg_cols, seg_width)
        + [pl.BlockSpec((1, tn), lambda i, j: (0, j))],
        out_specs=pl.BlockSpec((tm, tn), lambda i, j: (i, j)),
        out_shape=jax.ShapeDtypeStruct((T, N), F32),
        compiler_params=_cparams("parallel", "arbitrary"),
        name="proj_q",
    )(h, w, hgain)


def _proj_kv(h, w, layer, k_col, v_col, width, kgain, dest):
    T, D = h.shape
    tm, tk = _divisor_tile(T, ROW_TILE), _divisor_tile(width, COL_TILE // 2)
    assert k_col % tk == 0 and v_col % tk == 0
    batch, depth, rows, prev = dest
    assert batch * rows == T
    in_specs = [pl.BlockSpec((tm, D), lambda i, j: (i, 0)),
                pl.BlockSpec((None, D, tk), lambda i, j: (layer, 0, k_col // tk + j)),
                pl.BlockSpec((None, D, tk), lambda i, j: (layer, 0, v_col // tk + j)),
                pl.BlockSpec((1, HEAD_DIM), lambda i, j: (0, 0))]
    args = [h, w, w, kgain]
    if rows >= tm:
        per = rows // tm
        out_spec = pl.BlockSpec((1, None, tm, 2 * tk), lambda i, j: (i // per, layer, i % per, j))
    else:
        assert tm % rows == 0 and rows % SUBLANES == 0
        out_spec = pl.BlockSpec((tm // rows, None, rows, 2 * tk), lambda i, j: (i, layer, 0, j))
    aliases = {}
    if prev is not None:
        in_specs.append(pl.BlockSpec(memory_space=pl.ANY))
        args.append(prev)
        aliases = {len(args) - 1: 0}
    return pl.pallas_call(
        _proj_kv_kernel,
        grid=(T // tm, width // tk),
        in_specs=in_specs, out_specs=out_spec,
        out_shape=jax.ShapeDtypeStruct((batch, depth, rows, 2 * width), F32),
        input_output_aliases=aliases,
        compiler_params=_cparams("parallel", "arbitrary"),
        name="proj_kv",
    )(*args)


def _proj_plain(h, w, layer, seg_cols, seg_width):
    T, D, N, tm, tn = _proj_tiles(h, seg_cols, seg_width)
    return pl.pallas_call(
        _proj_plain_kernel,
        grid=(T // tm, N // tn),
        in_specs=_proj_in_specs(tm, D, tn, layer, seg_cols, seg_width),
        out_specs=pl.BlockSpec((tm, tn), lambda i, j: (i, j)),
        out_shape=jax.ShapeDtypeStruct((T, N), F32),
        compiler_params=_cparams("parallel", "arbitrary"),
        name="proj_plain",
    )(h, w)


def _proj_hgrn_gate(h, w, layer, col, width, lb):
    T, D, N, tm, tn = _proj_tiles(h, [col], width)
    out = pl.BlockSpec((tm, tn), lambda i, j: (i, j))
    return pl.pallas_call(
        _proj_hgrn_gate_kernel,
        grid=(T // tm, N // tn),
        in_specs=_proj_in_specs(tm, D, tn, layer, [col], width)
        + [pl.BlockSpec((1, tn), lambda i, j: (0, j))],
        out_specs=[out, out],
        out_shape=[jax.ShapeDtypeStruct((T, N), F32)] * 2,
        compiler_params=_cparams("parallel", "arbitrary"),
        name="proj_hgrn_gate",
    )(h, w, lb)


def _proj_fox_gate(h, w, layer, col, bias, seq):
    T, D = h.shape
    tm = _divisor_tile(T, ROW_TILE)
    assert (seq % tm == 0) or (tm % seq == 0 and seq & (seq - 1) == 0)
    assert col % LANES == 0
    out = pl.BlockSpec((tm, LANES), lambda i: (i, 0))
    return pl.pallas_call(
        functools.partial(_proj_fox_gate_kernel, seq=seq, tm=tm),
        grid=(T // tm,),
        in_specs=[pl.BlockSpec((tm, D), lambda i: (i, 0)),
                  pl.BlockSpec((None, D, LANES), lambda i: (layer, 0, col // LANES)),
                  pl.BlockSpec((1, LANES), lambda i: (0, 0))],
        out_specs=[out, out],
        out_shape=[jax.ShapeDtypeStruct((T, LANES), F32)] * 2,
        scratch_shapes=[pltpu.VMEM((1, LANES), F32)],
        compiler_params=_cparams("arbitrary"),
        name="proj_fox_gate",
    )(h, w, bias)


def _kv_block(kv_ref, off, blk, hh):
    k = kv_ref[pl.ds(off, blk), hh * 2 * HEAD_DIM:hh * 2 * HEAD_DIM + HEAD_DIM].astype(BF16)
    v = kv_ref[pl.ds(off, blk), hh * 2 * HEAD_DIM + HEAD_DIM:(hh + 1) * 2 * HEAD_DIM].astype(BF16)
    return k, v


def _fox_prompt_kernel(q_ref, kv_ref, ccol_ref, crow_ref, og_ref, o_ref, *, blk, hp):
    g = pl.program_id(1)
    i = pl.program_id(2)
    heads = range(hp)
    qs = [q_ref[:, hh * HEAD_DIM:(hh + 1) * HEAD_DIM].astype(BF16) for hh in heads]
    ccols = [_pick_lane(ccol_ref[...], g * hp + hh) for hh in heads]
    t_io = lax.broadcasted_iota(jnp.int32, (blk, blk), 0)
    s_io = lax.broadcasted_iota(jnp.int32, (blk, blk), 1)

    def block(j, carry, diagonal):
        off = pl.multiple_of(j * blk, blk)
        crows = crow_ref[:, pl.ds(off, blk)]
        out = []
        for hh in heads:
            m, l, acc = carry[hh]
            k, v = _kv_block(kv_ref, off, blk, hh)
            s = _dot_nt(qs[hh], k) + ccols[hh] - _pick_row(crows, g * hp + hh)
            if diagonal:
                s = jnp.where(s_io <= t_io, s, -jnp.inf)
            m_new = jnp.maximum(m, jnp.max(s, axis=1, keepdims=True))
            alpha = jnp.exp(m - m_new)
            p = jnp.exp(s - m_new)
            l = alpha * l + jnp.sum(p, axis=1, keepdims=True)
            acc = alpha * acc + _dot(p.astype(BF16), v)
            out.append((m_new, l, acc))
        return tuple(out)

    init = tuple((jnp.full((blk, 1), -jnp.inf, F32), jnp.zeros((blk, 1), F32),
                  jnp.zeros((blk, HEAD_DIM), F32)) for _ in heads)
    carry = lax.fori_loop(0, i, lambda j, c: block(j, c, False), init)
    carry = block(i, carry, True)
    for hh in heads:
        _, l, acc = carry[hh]
        sl = slice(hh * HEAD_DIM, (hh + 1) * HEAD_DIM)
        o_ref[:, sl] = _rms_rows(acc / l, og_ref[:, sl]).astype(BF16)


def _sb_prompt_kernel(q_ref, kv_ref, og_ref, o_ref, *, blk, hp):
    i = pl.program_id(2)
    heads = range(hp)
    qs = [q_ref[:, hh * HEAD_DIM:(hh + 1) * HEAD_DIM].astype(BF16) for hh in heads]
    later = _later_matrix(blk)
    t_io = lax.broadcasted_iota(jnp.int32, (blk, blk), 0)
    s_io = lax.broadcasted_iota(jnp.int32, (blk, blk), 1)
    strict = s_io < t_io

    def block(j, carry, diagonal):
        off = pl.multiple_of(j * blk, blk)
        out = []
        for hh in heads:
            run, acc = carry[hh]
            k, v = _kv_block(kv_ref, off, blk, hh)
            z = _dot_nt(qs[hh], k)
            lsm = _log_sigmoid(-z)
            if diagonal:
                lsm = jnp.where(strict, lsm, 0.0)
            a = jnp.exp(lsm + z + _dot2_lhs(lsm, later) + run)
            if diagonal:
                a = jnp.where(strict, a, 0.0)
            out.append((run + jnp.sum(lsm, axis=1, keepdims=True), acc + _dot(a.astype(BF16), v)))
        return tuple(out)

    def alive(carry):
        top = jnp.max(carry[0][0])
        for hh in range(1, hp):
            top = jnp.maximum(top, jnp.max(carry[hh][0]))
        return (top > EXP_UNDERFLOW).astype(jnp.int32)

    init = tuple((jnp.zeros((blk, 1), F32), jnp.zeros((blk, HEAD_DIM), F32)) for _ in heads)
    carry = block(i, init, True)

    def body(state):
        j, _, c = state
        c = block(j, c, False)
        return j - 1, alive(c), c

    _, _, carry = lax.while_loop(lambda st: jnp.logical_and(st[0] >= 0, st[1] > 0), body,
                                 (i - 1, alive(carry), carry))
    for hh in heads:
        sl = slice(hh * HEAD_DIM, (hh + 1) * HEAD_DIM)
        o_ref[:, sl] = _rms_rows(carry[hh][1], og_ref[:, sl]).astype(BF16)


def _attn_prompt_specs(S, blk, hp, layer, q_blk0, gain_blk0):
    qw, kvw = hp * HEAD_DIM, hp * 2 * HEAD_DIM
    q_spec = pl.BlockSpec((None, blk, qw), lambda b, g, i: (b, i, q_blk0 + g))
    kv_spec = pl.BlockSpec((None, None, S, kvw), lambda b, g, i: (b, layer, 0, g))
    gain_spec = pl.BlockSpec((1, qw), lambda b, g, i: (0, gain_blk0 + g))
    out_spec = pl.BlockSpec((None, blk, qw), lambda b, g, i: (b, i, g))
    return q_spec, kv_spec, gain_spec, out_spec


def _fox_prompt(q3, kv4, layer, cum3, cum_t, out_gain, n_heads):
    B, S, _ = q3.shape
    blk = _divisor_tile(S, ATTN_BLOCK)
    hp = _divisor_tile(n_heads, ATTN_HEADS_PER_STEP, align=1)
    q_spec, kv_spec, gain_spec, out_spec = _attn_prompt_specs(S, blk, hp, layer, 0, 0)
    return pl.pallas_call(
        functools.partial(_fox_prompt_kernel, blk=blk, hp=hp),
        grid=(B, n_heads // hp, S // blk),
        in_specs=[q_spec, kv_spec,
                  pl.BlockSpec((None, blk, LANES), lambda b, g, i: (b, i, 0)),
                  pl.BlockSpec((None, n_heads, S), lambda b, g, i: (b, 0, 0)),
                  gain_spec],
        out_specs=out_spec,
        out_shape=jax.ShapeDtypeStruct((B, S, n_heads * HEAD_DIM), BF16),
        compiler_params=_cparams("parallel", "parallel", "arbitrary"),
        name="fox_prompt",
    )(q3, kv4, cum3, cum_t, out_gain)


def _sb_prompt(q3, kv4, layer, out_gain, n_heads, q_head0, gain_head0):
    B, S, _ = q3.shape
    blk = _divisor_tile(S, ATTN_BLOCK)
    hp = _divisor_tile(n_heads, ATTN_HEADS_PER_STEP, align=1)
    assert q_head0 % hp == 0 and gain_head0 % hp == 0
    q_spec, kv_spec, gain_spec, out_spec = _attn_prompt_specs(S, blk, hp, layer, q_head0 // hp,
                                                              gain_head0 // hp)
    return pl.pallas_call(
        functools.partial(_sb_prompt_kernel, blk=blk, hp=hp),
        grid=(B, n_heads // hp, S // blk),
        in_specs=[q_spec, kv_spec, gain_spec],
        out_specs=out_spec,
        out_shape=jax.ShapeDtypeStruct((B, S, n_heads * HEAD_DIM), BF16),
        compiler_params=_cparams("parallel", "parallel", "arbitrary"),
        name="sb_prompt",
    )(q3, kv4, out_gain)


def _hgrn_kernel(*refs, chunk, heads, has_state):
    if has_state:
        q_ref, g_ref, k_ref, i_ref, gate_ref, og_ref, s0_ref, o_ref, sout_ref, st_ref = refs
    else:
        q_ref, g_ref, k_ref, i_ref, gate_ref, og_ref, o_ref, sout_ref, st_ref = refs
    c = pl.program_id(2)
    width = heads * HEAD_DIM
    groups = chunk // SUBLANES

    @pl.when(c == 0)
    def _():
        for hh in range(heads):
            if has_state:
                st_ref[hh] = s0_ref[hh].T
            else:
                st_ref[hh] = jnp.zeros((HEAD_DIM, HEAD_DIM), F32)

    q = q_ref[...]
    g = g_ref[...]
    kk = k_ref[...]
    iv = i_ref[...].astype(BF16)

    rig = lax.broadcasted_iota(jnp.int32, (SUBLANES, width), 0)
    zero_row = jnp.zeros((1, width), F32)
    gg, bg, carry = [], [], zero_row
    for gi in range(groups):
        rows = g[gi * SUBLANES:(gi + 1) * SUBLANES, :]
        local = jnp.zeros((SUBLANES, width), F32)
        for j in range(SUBLANES):
            local = local + jnp.where(rig >= j, rows[j:j + 1, :], 0.0)
        b_rows = local + carry
        carry = b_rows[SUBLANES - 1:SUBLANES, :]
        gg.append(rows)
        bg.append(b_rows)
    b_last = carry

    def b_at(gi, j):
        return zero_row if gi < 0 else bg[gi][j:j + 1, :]

    def pick(values):
        m = SUBLANES // len(values)
        out = values[-1]
        for k in range(len(values) - 2, -1, -1):
            out = jnp.where(rig < (k + 1) * m, values[k], out)
        return out

    levels = []
    m = chunk // 2
    while m >= 1:
        levels.append(m)
        m //= 2

    q_lvl, k_lvl = {}, {}
    for m in levels:
        q_rows, k_rows = [], []
        for gi in range(groups):
            qg = q[gi * SUBLANES:(gi + 1) * SUBLANES, :]
            kg = kk[gi * SUBLANES:(gi + 1) * SUBLANES, :]
            if m == 1:
                u, v = gg[gi], None
            elif m >= SUBLANES:
                r = m // SUBLANES
                first = r * (gi // r)
                u = bg[gi] - b_at(first - 1, SUBLANES - 1)
                v = b_at(first + r - 1, SUBLANES - 1) - bg[gi]
            else:
                n = SUBLANES // m
                prev = pick([b_at(gi - 1, SUBLANES - 1)] + [b_at(gi, k * m - 1) for k in range(1, n)])
                last = pick([b_at(gi, k * m + m - 1) for k in range(n)])
                u = bg[gi] - prev
                v = last - bg[gi]
            q_rows.append(qg * jnp.exp(u))
            k_rows.append(kg if v is None else kg * jnp.exp(v))
        q_lvl[m] = jnp.concatenate(q_rows, axis=0).astype(BF16)
        k_lvl[m] = jnp.concatenate(k_rows, axis=0).astype(BF16)

    b_all = jnp.concatenate(bg, axis=0)
    q_state = (q * jnp.exp(b_all)).astype(BF16)
    k_state = (kk * jnp.exp(b_last - b_all)).astype(BF16)
    decay = jnp.exp(b_last)
    q_bf = q.astype(BF16)
    k_bf = kk.astype(BF16)

    t_io = lax.broadcasted_iota(jnp.int32, (chunk, chunk), 0)
    s_io = lax.broadcasted_iota(jnp.int32, (chunk, chunk), 1)
    masks = {}
    for m in levels:
        sh = (2 * m).bit_length() - 1
        same = lax.shift_right_logical(t_io, sh) == lax.shift_right_logical(s_io, sh)
        upper = jnp.bitwise_and(t_io, 2 * m - 1) >= m
        lower = jnp.bitwise_and(s_io, 2 * m - 1) < m
        masks[m] = same & upper & lower

    for hh in range(heads):
        sl = slice(hh * HEAD_DIM, (hh + 1) * HEAD_DIM)
        w = jnp.where(t_io == s_io, _dot_nt(q_bf[:, sl], k_bf[:, sl]), 0.0)
        for m in levels:
            w = w + jnp.where(masks[m], _dot_nt(q_lvl[m][:, sl], k_lvl[m][:, sl]), 0.0)
        st = st_ref[hh]
        o = _dot(w.astype(BF16), iv[:, sl]) + _dot_nt(q_state[:, sl], st.astype(BF16))
        st_new = st * decay[:, sl] + _dot_tn(iv[:, sl], k_state[:, sl])
        st_ref[hh] = st_new
        gate = gate_ref[:, sl]
        o_ref[:, sl] = (_rms_rows(o, og_ref[:, sl]) * (gate * jax.nn.sigmoid(gate))).astype(BF16)

    @pl.when(c == pl.num_programs(2) - 1)
    def _():
        for hh in range(heads):
            sout_ref[hh] = st_ref[hh].T


def _hgrn(bm3, logf3, k3, out_gain, n_heads, gain_col0, s0):
    B, S, W = logf3.shape
    chunk = _divisor_tile(S, HG_CHUNK)
    hb = _divisor_tile(n_heads, HG_HEADS_PER_STEP, align=1)
    wb = hb * HEAD_DIM
    ng = n_heads // hb
    assert gain_col0 % wb == 0

    def col(k):
        return pl.BlockSpec((None, chunk, wb), lambda b, h, c: (b, c, k * ng + h))

    in_specs = [col(0),
                pl.BlockSpec((None, chunk, wb), lambda b, h, c: (b, c, h)),
                pl.BlockSpec((None, chunk, wb), lambda b, h, c: (b, c, h)),
                col(1), col(2),
                pl.BlockSpec((1, wb), lambda b, h, c: (0, gain_col0 // wb + h))]
    args = [bm3, logf3, k3, bm3, bm3, out_gain]
    state_spec = pl.BlockSpec((None, hb, HEAD_DIM, HEAD_DIM), lambda b, h, c: (b, h, 0, 0))
    if s0 is not None:
        in_specs.append(state_spec)
        args.append(s0)
    return pl.pallas_call(
        functools.partial(_hgrn_kernel, chunk=chunk, heads=hb, has_state=s0 is not None),
        grid=(B, ng, S // chunk),
        in_specs=in_specs,
        out_specs=[pl.BlockSpec((None, chunk, wb), lambda b, h, c: (b, c, h)), state_spec],
        out_shape=[jax.ShapeDtypeStruct((B, S, W), BF16),
                   jax.ShapeDtypeStruct((B, n_heads, HEAD_DIM, HEAD_DIM), F32)],
        scratch_shapes=[pltpu.VMEM((hb, HEAD_DIM, HEAD_DIM), F32)],
        compiler_params=_cparams("parallel", "parallel", "arbitrary"),
        name="hgrn",
    )(*args)


def _place_queries(q_ref, q16_ref, n_heads, n_new):
    q16_ref[...] = jnp.zeros_like(q16_ref)
    for h in range(n_heads):
        q16_ref[h, 0:n_new, :] = q_ref[:, h * HEAD_DIM:(h + 1) * HEAD_DIM].astype(BF16)


def _page_heads(k_ref, v_ref, n_heads, page_rows):
    ks = [k_ref[pl.ds(h, page_rows, stride=n_heads), :].astype(BF16) for h in range(n_heads)]
    vs = [v_ref[pl.ds(h, page_rows, stride=n_heads), :].astype(BF16) for h in range(n_heads)]
    return ks, vs


def _new_heads(kvnew_ref, n_heads, page_rows):
    new = kvnew_ref[...]
    pad = jnp.zeros((page_rows - new.shape[0], new.shape[1]), F32)
    kv = jnp.concatenate([new, pad], axis=0).astype(BF16)
    ks = [kv[:, h * 2 * HEAD_DIM:h * 2 * HEAD_DIM + HEAD_DIM] for h in range(n_heads)]
    vs = [kv[:, h * 2 * HEAD_DIM + HEAD_DIM:(h + 1) * 2 * HEAD_DIM] for h in range(n_heads)]
    return ks, vs


def _head_scores(q16_ref, ks, n_new):
    return jnp.concatenate([_dot_nt(q16_ref[h], k)[:n_new] for h, k in enumerate(ks)], axis=0)


def _head_values(p, vs, n_new, q_rows):
    outs = []
    for h, v in enumerate(vs):
        ph = p[h * n_new:(h + 1) * n_new, :]
        if q_rows > n_new:
            ph = jnp.concatenate([ph, jnp.zeros((q_rows - n_new, ph.shape[1]), F32)], axis=0)
        outs.append(_dot(ph.astype(BF16), v)[:n_new])
    return jnp.concatenate(outs, axis=0)


def _store_heads(o, og_ref, o_ref, n_heads, n_new):
    for h in range(n_heads):
        sl = slice(h * HEAD_DIM, (h + 1) * HEAD_DIM)
        o_ref[:, sl] = _rms_rows(o[h * n_new:(h + 1) * n_new, :], og_ref[:, sl]).astype(BF16)


def _fox_sample_kernel(pt_ref, q_ref, kvnew_ref, e_ref, et_ref, *rest, n_heads, n_new, page_rows, pps):
    del pt_ref
    k_refs, v_refs, lft_refs = rest[:pps], rest[pps:2 * pps], rest[2 * pps:3 * pps]
    og_ref, o_ref, q16_ref, m_ref, l_ref, acc_ref, run_ref, ecol_ref = rest[3 * pps:]
    p = pl.program_id(1)
    rows = n_heads * n_new
    q_rows = q16_ref.shape[1]

    def update(s, v_chunks):
        m_old = m_ref[...]
        m_new = jnp.maximum(m_old, jnp.max(s, axis=1, keepdims=True))
        alpha = jnp.exp(m_old - m_new)
        pm = jnp.exp(s - m_new)
        l_ref[...] = alpha * l_ref[...] + jnp.sum(pm, axis=1, keepdims=True)
        acc = alpha * acc_ref[...]
        for c, vs in enumerate(v_chunks):
            acc = acc + _head_values(pm[:, c * page_rows:(c + 1) * page_rows], vs, n_new, q_rows)
        acc_ref[...] = acc
        m_ref[...] = m_new

    @pl.when(p == 0)
    def _():
        _place_queries(q_ref, q16_ref, n_heads, n_new)
        m_ref[...] = jnp.full_like(m_ref, -jnp.inf)
        l_ref[...] = jnp.zeros_like(l_ref)
        acc_ref[...] = jnp.zeros_like(acc_ref)
        run_ref[...] = jnp.zeros_like(run_ref)
        e = e_ref[...]
        ecol = jnp.concatenate([_pick_lane(e, h) for h in range(n_heads)], axis=0)
        ecol_ref[...] = ecol
        row_io = lax.broadcasted_iota(jnp.int32, (rows, page_rows), 0)
        col_io = lax.broadcasted_iota(jnp.int32, (rows, page_rows), 1)
        ks, vs = _new_heads(kvnew_ref, n_heads, page_rows)
        s = _head_scores(q16_ref, ks, n_new) + ecol - _repeat_rows(et_ref[...], n_new)
        update(jnp.where(col_io <= jnp.bitwise_and(row_io, n_new - 1), s, -jnp.inf), [vs])

    later = _later_matrix(page_rows)
    run = run_ref[...]
    ecol = ecol_ref[...]
    s_chunks, v_chunks = [], []
    for r in range(pps):
        ks, vs = _page_heads(k_refs[r], v_refs[r], n_heads, page_rows)
        lf = _repeat_rows(lft_refs[r][...], n_new)
        s_chunks.append(_head_scores(q16_ref, ks, n_new) + ecol + _dot2_lhs(lf, later) + run)
        v_chunks.append(vs)
        run = run + jnp.sum(lf, axis=1, keepdims=True)
    run_ref[...] = run
    update(jnp.concatenate(s_chunks, axis=1), v_chunks)

    @pl.when(p == pl.num_programs(1) - 1)
    def _():
        _store_heads(acc_ref[...] / l_ref[...], og_ref, o_ref, n_heads, n_new)


def _sb_sample_kernel(pt_ref, q_ref, kvnew_ref, *rest, n_heads, n_new, page_rows, pps):
    del pt_ref
    k_refs, v_refs = rest[:pps], rest[pps:2 * pps]
    og_ref, o_ref, q16_ref, acc_ref, run_ref = rest[2 * pps:]
    p = pl.program_id(1)
    rows = n_heads * n_new
    q_rows = q16_ref.shape[1]
    later = _later_matrix(page_rows)

    def step(ks, vs, run, acc, mask):
        z = _head_scores(q16_ref, ks, n_new)
        lsm = _log_sigmoid(-z)
        if mask is not None:
            lsm = jnp.where(mask, lsm, 0.0)
        a = jnp.exp(lsm + z + _dot2_lhs(lsm, later) + run)
        if mask is not None:
            a = jnp.where(mask, a, 0.0)
        return run + jnp.sum(lsm, axis=1, keepdims=True), acc + _head_values(a, vs, n_new, q_rows)

    @pl.when(p == 0)
    def _():
        _place_queries(q_ref, q16_ref, n_heads, n_new)
        row_io = lax.broadcasted_iota(jnp.int32, (rows, page_rows), 0)
        col_io = lax.broadcasted_iota(jnp.int32, (rows, page_rows), 1)
        ks, vs = _new_heads(kvnew_ref, n_heads, page_rows)
        run, acc = step(ks, vs, jnp.zeros((rows, 1), F32), jnp.zeros((rows, HEAD_DIM), F32),
                        col_io < jnp.bitwise_and(row_io, n_new - 1))
        run_ref[...] = run
        acc_ref[...] = acc

    @pl.when(jnp.max(run_ref[...]) > EXP_UNDERFLOW)
    def _():
        run, acc = run_ref[...], acc_ref[...]
        for r in range(pps):
            ks, vs = _page_heads(k_refs[r], v_refs[r], n_heads, page_rows)
            run, acc = step(ks, vs, run, acc, None)
        run_ref[...] = run
        acc_ref[...] = acc

    @pl.when(p == pl.num_programs(1) - 1)
    def _():
        _store_heads(acc_ref[...], og_ref, o_ref, n_heads, n_new)


def _page_spec(shape, n_pages, pps, layer, r, col=0):
    def index(b, p, pt):
        return (pt[b * n_pages + n_pages - 1 - (p * pps + r)], layer, 0, col)
    return pl.BlockSpec((None, None) + shape, index)


def _kv_page_specs(rows, n_pages, pps, layer):
    return [_page_spec((rows, HEAD_DIM), n_pages, pps, layer, r, col)
            for col in (0, 1) for r in range(pps)]


def _decode_common(page_table, q3, cache_kv, n_heads):
    B, L, _ = q3.shape
    n_pages = page_table.shape[1]
    page_rows = cache_kv.shape[2] // n_heads
    pps = _divisor_tile(n_pages, PAGES_PER_STEP, align=1)
    assert L & (L - 1) == 0 and L % SUBLANES == 0 and L <= page_rows and page_rows == LANES
    q_rows = -(-L // BF16_ROWS) * BF16_ROWS
    return B, L, n_pages, page_rows, pps, q_rows


def _fox_sample(page_table, q3, kvnew4, cum3, cum_t, cache_kv, cache_lft, out_gain, layer, n_heads):
    B, L, n_pages, page_rows, pps, q_rows = _decode_common(page_table, q3, cache_kv, n_heads)
    rows = n_heads * L
    qw = n_heads * HEAD_DIM
    lf_shape = (n_heads, page_rows)
    grid_spec = pltpu.PrefetchScalarGridSpec(
        num_scalar_prefetch=1,
        grid=(B, n_pages // pps),
        in_specs=[pl.BlockSpec((None, L, qw), lambda b, p, pt: (b, 0, 0)),
                  pl.BlockSpec((None, None, L, 2 * qw), lambda b, p, pt: (b, layer, 0, 0)),
                  pl.BlockSpec((None, L, LANES), lambda b, p, pt: (b, 0, 0)),
                  pl.BlockSpec((None, n_heads, LANES), lambda b, p, pt: (b, 0, 0))]
        + _kv_page_specs(page_rows * n_heads, n_pages, pps, layer)
        + [_page_spec(lf_shape, n_pages, pps, layer, r) for r in range(pps)]
        + [pl.BlockSpec((1, qw), lambda b, p, pt: (0, 0))],
        out_specs=pl.BlockSpec((None, L, qw), lambda b, p, pt: (b, 0, 0)),
        scratch_shapes=[pltpu.VMEM((n_heads, q_rows, HEAD_DIM), BF16), pltpu.VMEM((rows, 1), F32),
                        pltpu.VMEM((rows, 1), F32), pltpu.VMEM((rows, HEAD_DIM), F32),
                        pltpu.VMEM((rows, 1), F32), pltpu.VMEM((rows, 1), F32)])
    return pl.pallas_call(
        functools.partial(_fox_sample_kernel, n_heads=n_heads, n_new=L, page_rows=page_rows, pps=pps),
        grid_spec=grid_spec,
        out_shape=jax.ShapeDtypeStruct((B, L, qw), BF16),
        compiler_params=_cparams("parallel", "arbitrary"),
        name="fox_sample",
    )(page_table.reshape(-1), q3, kvnew4, cum3, cum_t, *([cache_kv] * (2 * pps)),
      *([cache_lft] * pps), out_gain)


def _sb_sample(page_table, q3, kvnew4, cache_kv, out_gain, layer, n_heads):
    B, L, n_pages, page_rows, pps, q_rows = _decode_common(page_table, q3, cache_kv, n_heads)
    rows = n_heads * L
    qw = n_heads * HEAD_DIM
    grid_spec = pltpu.PrefetchScalarGridSpec(
        num_scalar_prefetch=1,
        grid=(B, n_pages // pps),
        in_specs=[pl.BlockSpec((None, L, qw), lambda b, p, pt: (b, 0, 1)),
                  pl.BlockSpec((None, None, L, 2 * qw), lambda b, p, pt: (b, layer, 0, 0))]
        + _kv_page_specs(page_rows * n_heads, n_pages, pps, layer)
        + [pl.BlockSpec((1, qw), lambda b, p, pt: (0, 1))],
        out_specs=pl.BlockSpec((None, L, qw), lambda b, p, pt: (b, 0, 0)),
        scratch_shapes=[pltpu.VMEM((n_heads, q_rows, HEAD_DIM), BF16),
                        pltpu.VMEM((rows, HEAD_DIM), F32), pltpu.VMEM((rows, 1), F32)])
    return pl.pallas_call(
        functools.partial(_sb_sample_kernel, n_heads=n_heads, n_new=L, page_rows=page_rows, pps=pps),
        grid_spec=grid_spec,
        out_shape=jax.ShapeDtypeStruct((B, L, qw), BF16),
        compiler_params=_cparams("parallel", "arbitrary"),
        name="sb_sample",
    )(page_table.reshape(-1), q3, kvnew4, *([cache_kv] * (2 * pps)), out_gain)


def _out_proj_kernel(x_ref, a_ref, b_ref, c_ref, wa_ref, wb_ref, wc_ref, o_ref):
    o_ref[...] = (x_ref[...] + _dot(a_ref[...], wa_ref[...]) + _dot(b_ref[...], wb_ref[...])
                  + _dot(c_ref[...], wc_ref[...]))


def _out_proj(x, oa, ob, oc, w, layer):
    T, D = x.shape
    tm, tn = _divisor_tile(T, ROW_TILE), _divisor_tile(D, COL_TILE)
    wa, wb, wc = oa.shape[1], ob.shape[1], oc.shape[1]
    assert wa % wb == 0 and (wa + wb) % wc == 0

    def act(a):
        return pl.BlockSpec((tm, a.shape[1]), lambda i, j: (i, 0))

    def band(rows, row_blk):
        return pl.BlockSpec((None, rows, tn), lambda i, j: (layer, row_blk, j))

    return pl.pallas_call(
        _out_proj_kernel,
        grid=(T // tm, D // tn),
        in_specs=[pl.BlockSpec((tm, tn), lambda i, j: (i, j)), act(oa), act(ob), act(oc),
                  band(wa, 0), band(wb, wa // wb), band(wc, (wa + wb) // wc)],
        out_specs=pl.BlockSpec((tm, tn), lambda i, j: (i, j)),
        out_shape=jax.ShapeDtypeStruct((T, D), F32),
        compiler_params=_cparams("parallel", "arbitrary"),
        name="out_proj",
    )(x, oa, ob, oc, w, w, w)


def _ple_kernel(x_ref, g_ref, wg_ref, p_ref, wp_ref, o_ref, h_ref, *, tn):
    j = pl.program_id(1)

    @pl.when(j == 0)
    def _():
        h_ref[...] = _rms_rows(x_ref[...], g_ref[...]).astype(BF16)

    gate = jax.nn.sigmoid(_dot(h_ref[...], wg_ref[...]))
    emb = _dot(p_ref[...].astype(BF16), wp_ref[...])
    o_ref[...] = x_ref[:, pl.ds(pl.multiple_of(j * tn, tn), tn)] + gate * emb


def _ple(x, gain, wg, p, wp, layer):
    T, D = x.shape
    tm, tn = _divisor_tile(T, ROW_TILE), _divisor_tile(D, COL_TILE)
    return pl.pallas_call(
        functools.partial(_ple_kernel, tn=tn),
        grid=(T // tm, D // tn),
        in_specs=[pl.BlockSpec((tm, D), lambda i, j: (i, 0)),
                  pl.BlockSpec((1, D), lambda i, j: (0, 0)),
                  pl.BlockSpec((None, D, tn), lambda i, j: (layer, 0, j)),
                  pl.BlockSpec((tm, p.shape[1]), lambda i, j: (i, 0)),
                  pl.BlockSpec((None, p.shape[1], tn), lambda i, j: (layer, 0, j))],
        out_specs=pl.BlockSpec((tm, tn), lambda i, j: (i, j)),
        out_shape=jax.ShapeDtypeStruct((T, D), F32),
        scratch_shapes=[pltpu.VMEM((tm, D), BF16)],
        compiler_params=_cparams("parallel", "arbitrary"),
        name="ple",
    )(x, gain, wg, p, wp)


def _aligned_w_in(w_in, hf, hs, hg):
    fw, sw, gw = hf * HEAD_DIM, hs * HEAD_DIM, hg * HEAD_DIM
    gate0 = 3 * fw
    w = jnp.concatenate([w_in[..., :gate0], w_in[..., gate0 + hf:],
                         jnp.pad(w_in[..., gate0:gate0 + hf], ((0, 0), (0, 0), (0, LANES - hf)))],
                        axis=-1).astype(BF16)
    cols, o = {}, 0
    for name, size in (("fq", fw), ("fk", fw), ("fv", fw), ("sq", sw), ("sk", sw), ("sv", sw),
                       ("hq", gw), ("hf", gw), ("hi", gw), ("hg", gw), ("ff", LANES)):
        cols[name] = o
        o += size
    return w, cols


def kernel(x_prompt, x_sample, p_prompt, p_sample, cache_fox_kv, cache_fox_logf, cache_sb_kv, state_hgrn,
           page_table, ffn1_norm, ffn1_w1, ffn1_w3, ffn1_w2, mix_norm, w_in, fox_f_bias, fox_q_norm,
           fox_k_norm, sb_q_norm, sb_k_norm, hg_lower_bound, out_norm, w_out, ffn2_norm, ffn2_w1, ffn2_w3,
           ffn2_w2, ple_norm, w_ple_gate, w_ple_proj):
    depth = w_in.shape[0]
    B, S, D = x_prompt.shape
    Bs, L, _ = x_sample.shape
    hf, hs = fox_f_bias.shape[1], cache_sb_kv.shape[3]
    hg = state_hgrn.shape[2]
    assert hf == hs
    fw, sw, gw = hf * HEAD_DIM, hs * HEAD_DIM, hg * HEAD_DIM
    n_pool, _, page_rows, _, _ = cache_fox_kv.shape

    w_in_b, cols = _aligned_w_in(w_in, hf, hs, hg)
    ffn1 = (ffn1_w1.astype(BF16), ffn1_w3.astype(BF16), ffn1_w2.astype(BF16))
    ffn2 = (ffn2_w1.astype(BF16), ffn2_w3.astype(BF16), ffn2_w2.astype(BF16))
    w_out_b, w_gate_b, w_emb_b = w_out.astype(BF16), w_ple_gate.astype(BF16), w_ple_proj.astype(BF16)

    gamma = jnp.cumsum(jax.nn.softmax(hg_lower_bound.astype(F32), axis=0), axis=0)
    cache_fkv = cache_fox_kv.reshape(n_pool, depth, page_rows * hf, 2 * HEAD_DIM)
    cache_skv = cache_sb_kv.reshape(n_pool, depth, page_rows * hs, 2 * HEAD_DIM)
    cache_lft = jnp.swapaxes(cache_fox_logf.astype(F32), 2, 3)

    xp = x_prompt.reshape(B * S, D)
    xs = x_sample.reshape(Bs * L, D)
    hg_pad = _divisor_tile(max(L, HG_CHUNK), HG_CHUNK)
    pf_kv = psb_kv = sf_kv = ssb_kv = None
    logf_p, logf_s, state_p, state_s = [], [], [], []
    scale = HEAD_DIM ** -0.5

    for l in range(depth):
        lb = (gamma[l - 1] if l > 0 else jnp.zeros_like(gamma[0]))[None, :]
        out_gain = out_norm[l][None, :]
        q_gain = jnp.concatenate([jnp.tile(fox_q_norm[l] * scale, hf),
                                  jnp.tile(sb_q_norm[l] * scale, hs)])[None, :]
        f_bias = jnp.pad(fox_f_bias[l], (0, LANES - hf))[None, :]

        def pre_mix(x, batch, seq, fkv_prev, skv_prev):
            x, h = _ffn(x, ffn1_norm[l][None, :], *ffn1, l, next_gain=mix_norm[l][None, :])
            fkv = _proj_kv(h, w_in_b, l, cols["fk"], cols["fv"], fw, fox_k_norm[l][None, :],
                           (batch, depth, seq, fkv_prev))
            skv = _proj_kv(h, w_in_b, l, cols["sk"], cols["sv"], sw, sb_k_norm[l][None, :],
                           (batch, depth, seq, skv_prev))
            q = _proj_q(h, w_in_b, l, [cols["fq"], cols["sq"]], fw, q_gain).reshape(batch, seq, fw + sw)
            bm = _proj_plain(h, w_in_b, l, [cols["hq"], cols["hi"], cols["hg"]], gw)
            glogf, gk = _proj_hgrn_gate(h, w_in_b, l, cols["hf"], gw, lb)
            flogf, fcum = _proj_fox_gate(h, w_in_b, l, cols["ff"], f_bias, seq)
            return (x, fkv, skv, q, bm.reshape(batch, seq, 3 * gw), glogf.reshape(batch, seq, gw),
                    gk.reshape(batch, seq, gw), flogf.reshape(batch, seq, LANES),
                    fcum.reshape(batch, seq, LANES))

        def post_mix(x, o_fox, o_sb, o_hg, p_l):
            x = _out_proj(x, o_fox, o_sb, o_hg, w_out_b, l)
            x = _ffn(x, ffn2_norm[l][None, :], *ffn2, l)
            return _ple(x, ple_norm[l][None, :], w_gate_b, p_l, w_emb_b, l)

        xp, pf_kv, psb_kv, q, bm, glogf, gk, flogf, fcum = pre_mix(xp, B, S, pf_kv, psb_kv)
        cum_t = jnp.swapaxes(fcum[:, :, :hf], 1, 2)
        o_fox = _fox_prompt(q, pf_kv, l, fcum, cum_t, out_gain, hf)
        o_sb = _sb_prompt(q, psb_kv, l, out_gain, hs, hf, hf)
        o_hg, st = _hgrn(bm, glogf, gk, out_gain, hg, fw + sw, None)
        xp = post_mix(xp, o_fox.reshape(B * S, fw), o_sb.reshape(B * S, sw), o_hg.reshape(B * S, gw),
                      p_prompt[l].reshape(B * S, -1))
        logf_p.append(flogf[:, :, :hf])
        state_p.append(st)

        xs, sf_kv, ssb_kv, q, bm, glogf, gk, flogf, fcum = pre_mix(xs, Bs, L, sf_kv, ssb_kv)
        cum_t = jnp.pad(jnp.swapaxes(fcum[:, :, :hf], 1, 2), ((0, 0), (0, 0), (0, LANES - L)))
        o_fox = _fox_sample(page_table, q, sf_kv, fcum, cum_t, cache_fkv, cache_lft, out_gain, l, hf)
        o_sb = _sb_sample(page_table, q, ssb_kv, cache_skv, out_gain, l, hs)

        def pad_tokens(z):
            return jnp.pad(z, ((0, 0), (0, hg_pad - L), (0, 0)))

        o_hg, st = _hgrn(pad_tokens(bm), pad_tokens(glogf), pad_tokens(gk), out_gain, hg, fw + sw,
                         state_hgrn[l].astype(F32))
        xs = post_mix(xs, o_fox.reshape(Bs * L, fw), o_sb.reshape(Bs * L, sw),
                      o_hg[:, :L].reshape(Bs * L, gw), p_sample[l].reshape(Bs * L, -1))
        logf_s.append(flogf[:, :, :hf])
        state_s.append(st)

    return (xp.reshape(B, S, D), xs.reshape(Bs, L, D),
            pf_kv.reshape(B, depth, S, hf, 2 * HEAD_DIM), jnp.stack(logf_p, axis=1),
            psb_kv.reshape(B, depth, S, hs, 2 * HEAD_DIM), jnp.stack(state_p, axis=0),
            sf_kv.reshape(Bs, depth, L, hf, 2 * HEAD_DIM), jnp.stack(logf_s, axis=1),
            ssb_kv.reshape(Bs, depth, L, hs, 2 * HEAD_DIM), jnp.stack(state_s, axis=0))
```

```python
import functools

import jax
import jax.numpy as jnp
from jax import lax
from jax.experimental import pallas as pl
from jax.experimental.pallas import tpu as pltpu

F32 = jnp.float32
BF16 = jnp.bfloat16

HEAD_DIM = 128
EPS = 1e-6
LANES = 128
SUBLANES = 8
BF16_ROWS = 16
VMEM_LIMIT_BYTES = 56 << 20
VMEM_LIMIT_FFN_BYTES = 60 << 20
ROW_TILE = 512
COL_TILE = 1024
FF_TILE = 256
ATTN_BLOCK = 256
ATTN_HEADS_PER_STEP = 4
HG_CHUNK = 128
HG_HEADS_PER_STEP = 4
PAGES_PER_STEP = 8
EXP_UNDERFLOW = -104.0


def _cparams(*sem, vmem=VMEM_LIMIT_BYTES):
    return pltpu.CompilerParams(dimension_semantics=sem, vmem_limit_bytes=vmem)


def _divisor_tile(n, pref, align=LANES):
    if n <= pref:
        return n
    for t in range(pref - pref % align, 0, -align):
        if n % t == 0:
            return t
    raise ValueError((n, pref, align))


def _dot(a, b):
    return jnp.dot(a, b, preferred_element_type=F32)


def _dot_nt(a, b):
    return lax.dot_general(a, b, (((1,), (1,)), ((), ())), preferred_element_type=F32)


def _dot_tn(a, b):
    return lax.dot_general(a, b, (((0,), (0,)), ((), ())), preferred_element_type=F32)


def _dot3_rhs(m_bf16, x):
    hi = x.astype(BF16)
    r1 = x - hi.astype(F32)
    mid = r1.astype(BF16)
    lo = (r1 - mid.astype(F32)).astype(BF16)
    return _dot(m_bf16, hi) + _dot(m_bf16, mid) + _dot(m_bf16, lo)


def _dot2_lhs(x, m_bf16):
    hi = x.astype(BF16)
    lo = (x - hi.astype(F32)).astype(BF16)
    return _dot(hi, m_bf16) + _dot(lo, m_bf16)


def _log_sigmoid(x):
    return jnp.minimum(x, 0.0) - jnp.log1p(jnp.exp(-jnp.abs(x)))


def _rms_rows(x, gain):
    return x * lax.rsqrt(jnp.mean(x * x, axis=-1, keepdims=True) + EPS) * gain


def _ones_where(cond):
    return jnp.where(cond, 1.0, 0.0).astype(BF16)


def _later_matrix(n):
    j_io = lax.broadcasted_iota(jnp.int32, (n, n), 0)
    s_io = lax.broadcasted_iota(jnp.int32, (n, n), 1)
    return _ones_where(j_io > s_io)


def _pick_lane(x, idx):
    lane = lax.broadcasted_iota(jnp.int32, x.shape, 1)
    return jnp.sum(jnp.where(lane == idx, x, 0.0), axis=1, keepdims=True)


def _pick_row(x, idx):
    row = lax.broadcasted_iota(jnp.int32, x.shape, 0)
    return jnp.sum(jnp.where(row == idx, x, 0.0), axis=0, keepdims=True)


def _repeat_rows(x, reps):
    return jnp.concatenate(
        [jnp.broadcast_to(x[h:h + 1, :], (reps, x.shape[1])) for h in range(x.shape[0])], axis=0)


def _ffn_kernel(*refs, emit_norm):
    if emit_norm:
        x_ref, g_ref, w1_ref, w3_ref, w2_ref, g2_ref, o_ref, h2_ref, h_ref = refs
    else:
        x_ref, g_ref, w1_ref, w3_ref, w2_ref, o_ref, h_ref = refs

    @pl.when(pl.program_id(1) == 0)
    def _():
        x = x_ref[...]
        h_ref[...] = _rms_rows(x, g_ref[...]).astype(BF16)
        o_ref[...] = x

    h = h_ref[...]
    a = _dot(h, w1_ref[...])
    b = _dot(h, w3_ref[...])
    gated = (a * jax.nn.sigmoid(a) * b * 0.5).astype(BF16)
    o_ref[...] += _dot(gated, w2_ref[...])

    if emit_norm:
        @pl.when(pl.program_id(1) == pl.num_programs(1) - 1)
        def _():
            h2_ref[...] = _rms_rows(o_ref[...], g2_ref[...]).astype(BF16)


def _ffn(x, gain, w1, w3, w2, layer, next_gain=None):
    T, D = x.shape
    F = w1.shape[2]
    tm = _divisor_tile(T, ROW_TILE)
    tf = _divisor_tile(F, FF_TILE)
    emit = next_gain is not None
    row = pl.BlockSpec((1, D), lambda i, j: (0, 0))
    tile = pl.BlockSpec((tm, D), lambda i, j: (i, 0))
    in_specs = [pl.BlockSpec((tm, D), lambda i, j: (i, 0), pipeline_mode=pl.Buffered(1)), row,
                pl.BlockSpec((None, D, tf), lambda i, j: (layer, 0, j)),
                pl.BlockSpec((None, D, tf), lambda i, j: (layer, 0, j)),
                pl.BlockSpec((None, tf, D), lambda i, j: (layer, j, 0))]
    args = [x, gain, w1, w3, w2]
    out_specs, out_shape = tile, jax.ShapeDtypeStruct((T, D), F32)
    if emit:
        in_specs.append(row)
        args.append(next_gain)
        out_specs = [tile, tile]
        out_shape = [out_shape, jax.ShapeDtypeStruct((T, D), BF16)]
    return pl.pallas_call(
        functools.partial(_ffn_kernel, emit_norm=emit),
        grid=(T // tm, F // tf),
        in_specs=in_specs, out_specs=out_specs, out_shape=out_shape,
        scratch_shapes=[pltpu.VMEM((tm, D), BF16)],
        compiler_params=_cparams("parallel", "arbitrary", vmem=VMEM_LIMIT_FFN_BYTES),
        name="ffn",
    )(*args)


def _proj_q_kernel(h_ref, w_ref, hgain_ref, o_ref):
    z = _dot(h_ref[...], w_ref[...])
    for c in range(z.shape[1] // HEAD_DIM):
        sl = slice(c * HEAD_DIM, (c + 1) * HEAD_DIM)
        o_ref[:, sl] = _rms_rows(z[:, sl], hgain_ref[:, sl])


def _proj_kv_kernel(*refs):
    h_ref, wk_ref, wv_ref, kgain_ref = refs[:4]
    o_ref = refs[-1]
    h = h_ref[...]
    zk = _dot(h, wk_ref[...])
    zv = _dot(h, wv_ref[...])
    lead = o_ref.shape[:-1]
    for c in range(zk.shape[1] // HEAD_DIM):
        sl = slice(c * HEAD_DIM, (c + 1) * HEAD_DIM)
        k_out = slice(2 * c * HEAD_DIM, (2 * c + 1) * HEAD_DIM)
        v_out = slice((2 * c + 1) * HEAD_DIM, (2 * c + 2) * HEAD_DIM)
        o_ref[..., k_out] = _rms_rows(zk[:, sl], kgain_ref[...]).reshape(*lead, HEAD_DIM)
        o_ref[..., v_out] = zv[:, sl].reshape(*lead, HEAD_DIM)


def _proj_plain_kernel(h_ref, w_ref, o_ref):
    o_ref[...] = _dot(h_ref[...], w_ref[...])


def _proj_hgrn_gate_kernel(h_ref, w_ref, lb_ref, logf_ref, k_ref):
    z = _dot(h_ref[...], w_ref[...])
    lb = lb_ref[...]
    a = jnp.log1p(-lb) + _log_sigmoid(z)
    b = jnp.log(lb)
    logf_ref[...] = jnp.maximum(a, b) + jnp.log1p(jnp.exp(-jnp.abs(a - b)))
    k_ref[...] = (1.0 - lb) * jax.nn.sigmoid(-z)


def _proj_fox_gate_kernel(h_ref, w_ref, bias_ref, logf_ref, cum_ref, carry_ref, *, seq, tm):
    i = pl.program_id(0)
    logf = _log_sigmoid(_dot(h_ref[...], w_ref[...]) + bias_ref[...])
    logf_ref[...] = logf
    t = lax.broadcasted_iota(jnp.int32, (tm, tm), 0)
    s = lax.broadcasted_iota(jnp.int32, (tm, tm), 1)
    if seq >= tm:
        @pl.when(i % (seq // tm) == 0)
        def _():
            carry_ref[...] = jnp.zeros_like(carry_ref)

        cum = _dot3_rhs(_ones_where(s <= t), logf) + carry_ref[...]
        carry_ref[...] = cum[tm - 1:tm, :]
    else:
        shift = seq.bit_length() - 1
        same = lax.shift_right_logical(t, shift) == lax.shift_right_logical(s, shift)
        cum = _dot3_rhs(_ones_where(same & (s <= t)), logf)
    cum_ref[...] = cum


def _proj_in_specs(tm, D, tn, layer, seg_cols, seg_width):
    per = seg_width // tn
    starts = [c // tn for c in seg_cols]
    assert seg_width % tn == 0 and all(c % tn == 0 for c in seg_cols)

    def w_index(i, j):
        blk = starts[-1]
        for k in range(len(starts) - 2, -1, -1):
            blk = jnp.where(j // per == k, starts[k], blk)
        return (layer, 0, blk + j % per)

    return [pl.BlockSpec((tm, D), lambda i, j: (i, 0)), pl.BlockSpec((None, D, tn), w_index)]


def _proj_tiles(h, seg_cols, seg_width):
    T, D = h.shape
    N = len(seg_cols) * seg_width
    return T, D, N, _divisor_tile(T, ROW_TILE), _divisor_tile(seg_width, COL_TILE)


def _proj_q(h, w, layer, seg_cols, seg_width, hgain):
    T, D, N, tm, tn = _proj_tiles(h, seg_cols, seg_width)
    return pl.pallas_call(
        _proj_q_kernel,
        grid=(T // tm, N // tn),
        in_specs=_proj_in_specs(tm, D, tn, layer, seg_cols, seg_width)
        + [pl.BlockSpec((1, tn), lambda i, j: (0, j))],
        out_specs=pl.BlockSpec((tm, tn), lambda i, j: (i, j)),
        out_shape=jax.ShapeDtypeStruct((T, N), F32),
        compiler_params=_cparams("parallel", "arbitrary"),
        name="proj_q",
    )(h, w, hgain)


def _proj_kv(h, w, layer, k_col, v_col, width, kgain, dest):
    T, D = h.shape
    tm, tk = _divisor_tile(T, ROW_TILE), _divisor_tile(width, COL_TILE // 2)
    assert k_col % tk == 0 and v_col % tk == 0
    batch, depth, rows, prev = dest
    assert batch * rows == T
    in_specs = [pl.BlockSpec((tm, D), lambda i, j: (i, 0)),
                pl.BlockSpec((None, D, tk), lambda i, j: (layer, 0, k_col // tk + j)),
                pl.BlockSpec((None, D, tk), lambda i, j: (layer, 0, v_col // tk + j)),
                pl.BlockSpec((1, HEAD_DIM), lambda i, j: (0, 0))]
    args = [h, w, w, kgain]
    if rows >= tm:
        per = rows // tm
        out_spec = pl.BlockSpec((1, None, tm, 2 * tk), lambda i, j: (i // per, layer, i % per, j))
    else:
        assert tm % rows == 0 and rows % SUBLANES == 0
        out_spec = pl.BlockSpec((tm // rows, None, rows, 2 * tk), lambda i, j: (i, layer, 0, j))
    aliases = {}
    if prev is not None:
        in_specs.append(pl.BlockSpec(memory_space=pl.ANY))
        args.append(prev)
        aliases = {len(args) - 1: 0}
    return pl.pallas_call(
        _proj_kv_kernel,
        grid=(T // tm, width // tk),
        in_specs=in_specs, out_specs=out_spec,
        out_shape=jax.ShapeDtypeStruct((batch, depth, rows, 2 * width), F32),
        input_output_aliases=aliases,
        compiler_params=_cparams("parallel", "arbitrary"),
        name="proj_kv",
    )(*args)


def _proj_plain(h, w, layer, seg_cols, seg_width):
    T, D, N, tm, tn = _proj_tiles(h, seg_cols, seg_width)
    return pl.pallas_call(
        _proj_plain_kernel,
        grid=(T // tm, N // tn),
        in_specs=_proj_in_specs(tm, D, tn, layer, seg_cols, seg_width),
        out_specs=pl.BlockSpec((tm, tn), lambda i, j: (i, j)),
        out_shape=jax.ShapeDtypeStruct((T, N), F32),
        compiler_params=_cparams("parallel", "arbitrary"),
        name="proj_plain",
    )(h, w)


def _proj_hgrn_gate(h, w, layer, col, width, lb):
    T, D, N, tm, tn = _proj_tiles(h, [col], width)
    out = pl.BlockSpec((tm, tn), lambda i, j: (i, j))
    return pl.pallas_call(
        _proj_hgrn_gate_kernel,
        grid=(T // tm, N // tn),
        in_specs=_proj_in_specs(tm, D, tn, layer, [col], width)
        + [pl.BlockSpec((1, tn), lambda i, j: (0, j))],
        out_specs=[out, out],
        out_shape=[jax.ShapeDtypeStruct((T, N), F32)] * 2,
        compiler_params=_cparams("parallel", "arbitrary"),
        name="proj_hgrn_gate",
    )(h, w, lb)


def _proj_fox_gate(h, w, layer, col, bias, seq):
    T, D = h.shape
    tm = _divisor_tile(T, ROW_TILE)
    assert (seq % tm == 0) or (tm % seq == 0 and seq & (seq - 1) == 0)
    assert col % LANES == 0
    out = pl.BlockSpec((tm, LANES), lambda i: (i, 0))
    return pl.pallas_call(
        functools.partial(_proj_fox_gate_kernel, seq=seq, tm=tm),
        grid=(T // tm,),
        in_specs=[pl.BlockSpec((tm, D), lambda i: (i, 0)),
                  pl.BlockSpec((None, D, LANES), lambda i: (layer, 0, col // LANES)),
                  pl.BlockSpec((1, LANES), lambda i: (0, 0))],
        out_specs=[out, out],
        out_shape=[jax.ShapeDtypeStruct((T, LANES), F32)] * 2,
        scratch_shapes=[pltpu.VMEM((1, LANES), F32)],
        compiler_params=_cparams("arbitrary"),
        name="proj_fox_gate",
    )(h, w, bias)


def _kv_block(kv_ref, off, blk, hh):
    k = kv_ref[pl.ds(off, blk), hh * 2 * HEAD_DIM:hh * 2 * HEAD_DIM + HEAD_DIM].astype(BF16)
    v = kv_ref[pl.ds(off, blk), hh * 2 * HEAD_DIM + HEAD_DIM:(hh + 1) * 2 * HEAD_DIM].astype(BF16)
    return k, v


def _fox_prompt_kernel(q_ref, kv_ref, ccol_ref, crow_ref, og_ref, o_ref, *, blk, hp):
    g = pl.program_id(1)
    i = pl.program_id(2)
    heads = range(hp)
    qs = [q_ref[:, hh * HEAD_DIM:(hh + 1) * HEAD_DIM].astype(BF16) for hh in heads]
    ccols = [_pick_lane(ccol_ref[...], g * hp + hh) for hh in heads]
    t_io = lax.broadcasted_iota(jnp.int32, (blk, blk), 0)
    s_io = lax.broadcasted_iota(jnp.int32, (blk, blk), 1)

    def block(j, carry, diagonal):
        off = pl.multiple_of(j * blk, blk)
        crows = crow_ref[:, pl.ds(off, blk)]
        out = []
        for hh in heads:
            m, l, acc = carry[hh]
            k, v = _kv_block(kv_ref, off, blk, hh)
            s = _dot_nt(qs[hh], k) + ccols[hh] - _pick_row(crows, g * hp + hh)
            if diagonal:
                s = jnp.where(s_io <= t_io, s, -jnp.inf)
            m_new = jnp.maximum(m, jnp.max(s, axis=1, keepdims=True))
            alpha = jnp.exp(m - m_new)
            p = jnp.exp(s - m_new)
            l = alpha * l + jnp.sum(p, axis=1, keepdims=True)
            acc = alpha * acc + _dot(p.astype(BF16), v)
            out.append((m_new, l, acc))
        return tuple(out)

    init = tuple((jnp.full((blk, 1), -jnp.inf, F32), jnp.zeros((blk, 1), F32),
                  jnp.zeros((blk, HEAD_DIM), F32)) for _ in heads)
    carry = lax.fori_loop(0, i, lambda j, c: block(j, c, False), init)
    carry = block(i, carry, True)
    for hh in heads:
        _, l, acc = carry[hh]
        sl = slice(hh * HEAD_DIM, (hh + 1) * HEAD_DIM)
        o_ref[:, sl] = _rms_rows(acc / l, og_ref[:, sl]).astype(BF16)


def _sb_prompt_kernel(q_ref, kv_ref, og_ref, o_ref, *, blk, hp):
    i = pl.program_id(2)
    heads = range(hp)
    qs = [q_ref[:, hh * HEAD_DIM:(hh + 1) * HEAD_DIM].astype(BF16) for hh in heads]
    later = _later_matrix(blk)
    t_io = lax.broadcasted_iota(jnp.int32, (blk, blk), 0)
    s_io = lax.broadcasted_iota(jnp.int32, (blk, blk), 1)
    strict = s_io < t_io

    def block(j, carry, diagonal):
        off = pl.multiple_of(j * blk, blk)
        out = []
        for hh in heads:
            run, acc = carry[hh]
            k, v = _kv_block(kv_ref, off, blk, hh)
            z = _dot_nt(qs[hh], k)
            lsm = _log_sigmoid(-z)
            if diagonal:
                lsm = jnp.where(strict, lsm, 0.0)
            a = jnp.exp(lsm + z + _dot2_lhs(lsm, later) + run)
            if diagonal:
                a = jnp.where(strict, a, 0.0)
            out.append((run + jnp.sum(lsm, axis=1, keepdims=True), acc + _dot(a.astype(BF16), v)))
        return tuple(out)

    def alive(carry):
        top = jnp.max(carry[0][0])
        for hh in range(1, hp):
            top = jnp.maximum(top, jnp.max(carry[hh][0]))
        return (top > EXP_UNDERFLOW).astype(jnp.int32)

    init = tuple((jnp.zeros((blk, 1), F32), jnp.zeros((blk, HEAD_DIM), F32)) for _ in heads)
    carry = block(i, init, True)

    def body(state):
        j, _, c = state
        c = block(j, c, False)
        return j - 1, alive(c), c

    _, _, carry = lax.while_loop(lambda st: jnp.logical_and(st[0] >= 0, st[1] > 0), body,
                                 (i - 1, alive(carry), carry))
    for hh in heads:
        sl = slice(hh * HEAD_DIM, (hh + 1) * HEAD_DIM)
        o_ref[:, sl] = _rms_rows(carry[hh][1], og_ref[:, sl]).astype(BF16)


def _attn_prompt_specs(S, blk, hp, layer, q_blk0, gain_blk0):
    qw, kvw = hp * HEAD_DIM, hp * 2 * HEAD_DIM
    q_spec = pl.BlockSpec((None, blk, qw), lambda b, g, i: (b, i, q_blk0 + g))
    kv_spec = pl.BlockSpec((None, None, S, kvw), lambda b, g, i: (b, layer, 0, g))
    gain_spec = pl.BlockSpec((1, qw), lambda b, g, i: (0, gain_blk0 + g))
    out_spec = pl.BlockSpec((None, blk, qw), lambda b, g, i: (b, i, g))
    return q_spec, kv_spec, gain_spec, out_spec


def _fox_prompt(q3, kv4, layer, cum3, cum_t, out_gain, n_heads):
    B, S, _ = q3.shape
    blk = _divisor_tile(S, ATTN_BLOCK)
    hp = _divisor_tile(n_heads, ATTN_HEADS_PER_STEP, align=1)
    q_spec, kv_spec, gain_spec, out_spec = _attn_prompt_specs(S, blk, hp, layer, 0, 0)
    return pl.pallas_call(
        functools.partial(_fox_prompt_kernel, blk=blk, hp=hp),
        grid=(B, n_heads // hp, S // blk),
        in_specs=[q_spec, kv_spec,
                  pl.BlockSpec((None, blk, LANES), lambda b, g, i: (b, i, 0)),
                  pl.BlockSpec((None, n_heads, S), lambda b, g, i: (b, 0, 0)),
                  gain_spec],
        out_specs=out_spec,
        out_shape=jax.ShapeDtypeStruct((B, S, n_heads * HEAD_DIM), BF16),
        compiler_params=_cparams("parallel", "parallel", "arbitrary"),
        name="fox_prompt",
    )(q3, kv4, cum3, cum_t, out_gain)


def _sb_prompt(q3, kv4, layer, out_gain, n_heads, q_head0, gain_head0):
    B, S, _ = q3.shape
    blk = _divisor_tile(S, ATTN_BLOCK)
    hp = _divisor_tile(n_heads, ATTN_HEADS_PER_STEP, align=1)
    assert q_head0 % hp == 0 and gain_head0 % hp == 0
    q_spec, kv_spec, gain_spec, out_spec = _attn_prompt_specs(S, blk, hp, layer, q_head0 // hp,
                                                              gain_head0 // hp)
    return pl.pallas_call(
        functools.partial(_sb_prompt_kernel, blk=blk, hp=hp),
        grid=(B, n_heads // hp, S // blk),
        in_specs=[q_spec, kv_spec, gain_spec],
        out_specs=out_spec,
        out_shape=jax.ShapeDtypeStruct((B, S, n_heads * HEAD_DIM), BF16),
        compiler_params=_cparams("parallel", "parallel", "arbitrary"),
        name="sb_prompt",
    )(q3, kv4, out_gain)


def _hgrn_kernel(*refs, chunk, heads, has_state):
    if has_state:
        q_ref, g_ref, k_ref, i_ref, gate_ref, og_ref, s0_ref, o_ref, sout_ref, st_ref = refs
    else:
        q_ref, g_ref, k_ref, i_ref, gate_ref, og_ref, o_ref, sout_ref, st_ref = refs
    c = pl.program_id(2)
    width = heads * HEAD_DIM
    groups = chunk // SUBLANES

    @pl.when(c == 0)
    def _():
        for hh in range(heads):
            if has_state:
                st_ref[hh] = s0_ref[hh].T
            else:
                st_ref[hh] = jnp.zeros((HEAD_DIM, HEAD_DIM), F32)

    q = q_ref[...]
    g = g_ref[...]
    kk = k_ref[...]
    iv = i_ref[...].astype(BF16)

    rig = lax.broadcasted_iota(jnp.int32, (SUBLANES, width), 0)
    zero_row = jnp.zeros((1, width), F32)
    gg, bg, carry = [], [], zero_row
    for gi in range(groups):
        rows = g[gi * SUBLANES:(gi + 1) * SUBLANES, :]
        local = jnp.zeros((SUBLANES, width), F32)
        for j in range(SUBLANES):
            local = local + jnp.where(rig >= j, rows[j:j + 1, :], 0.0)
        b_rows = local + carry
        carry = b_rows[SUBLANES - 1:SUBLANES, :]
        gg.append(rows)
        bg.append(b_rows)
    b_last = carry

    def b_at(gi, j):
        return zero_row if gi < 0 else bg[gi][j:j + 1, :]

    def pick(values):
        m = SUBLANES // len(values)
        out = values[-1]
        for k in range(len(values) - 2, -1, -1):
            out = jnp.where(rig < (k + 1) * m, values[k], out)
        return out

    levels = []
    m = chunk // 2
    while m >= 1:
        levels.append(m)
        m //= 2

    q_lvl, k_lvl = {}, {}
    for m in levels:
        q_rows, k_rows = [], []
        for gi in range(groups):
            qg = q[gi * SUBLANES:(gi + 1) * SUBLANES, :]
            kg = kk[gi * SUBLANES:(gi + 1) * SUBLANES, :]
            if m == 1:
                u, v = gg[gi], None
            elif m >= SUBLANES:
                r = m // SUBLANES
                first = r * (gi // r)
                if (gi // r) % 2 == 1:
                    q_rows.append(qg * jnp.exp(bg[gi] - b_at(first - 1, SUBLANES - 1)))
                    k_rows.append(jnp.zeros_like(kg))
                else:
                    q_rows.append(jnp.zeros_like(qg))
                    k_rows.append(kg * jnp.exp(b_at(first + r - 1, SUBLANES - 1) - bg[gi]))
                continue
            else:
                n = SUBLANES // m
                prev = pick([b_at(gi - 1, SUBLANES - 1)] + [b_at(gi, k * m - 1) for k in range(1, n)])
                last = pick([b_at(gi, k * m + m - 1) for k in range(n)])
                u = bg[gi] - prev
                v = last - bg[gi]
            q_rows.append(qg * jnp.exp(u))
            k_rows.append(kg if v is None else kg * jnp.exp(v))
        q_lvl[m] = jnp.concatenate(q_rows, axis=0).astype(BF16)
        k_lvl[m] = jnp.concatenate(k_rows, axis=0).astype(BF16)

    b_all = jnp.concatenate(bg, axis=0)
    q_state = (q * jnp.exp(b_all)).astype(BF16)
    k_state = (kk * jnp.exp(b_last - b_all)).astype(BF16)
    decay = jnp.exp(b_last)
    q_bf = q.astype(BF16)
    k_bf = kk.astype(BF16)

    t_io = lax.broadcasted_iota(jnp.int32, (chunk, chunk), 0)
    s_io = lax.broadcasted_iota(jnp.int32, (chunk, chunk), 1)
    masks = {}
    for m in levels:
        sh = (2 * m).bit_length() - 1
        same = lax.shift_right_logical(t_io, sh) == lax.shift_right_logical(s_io, sh)
        upper = jnp.bitwise_and(t_io, 2 * m - 1) >= m
        lower = jnp.bitwise_and(s_io, 2 * m - 1) < m
        masks[m] = same & upper & lower

    for hh in range(heads):
        sl = slice(hh * HEAD_DIM, (hh + 1) * HEAD_DIM)
        w = jnp.where(t_io == s_io, _dot_nt(q_bf[:, sl], k_bf[:, sl]), 0.0)
        for m in levels:
            w = w + jnp.where(masks[m], _dot_nt(q_lvl[m][:, sl], k_lvl[m][:, sl]), 0.0)
        st = st_ref[hh]
        o = _dot(w.astype(BF16), iv[:, sl]) + _dot_nt(q_state[:, sl], st.astype(BF16))
        st_new = st * decay[:, sl] + _dot_tn(iv[:, sl], k_state[:, sl])
        st_ref[hh] = st_new
        gate = gate_ref[:, sl]
        o_ref[:, sl] = (_rms_rows(o, og_ref[:, sl]) * (gate * jax.nn.sigmoid(gate))).astype(BF16)

    @pl.when(c == pl.num_programs(2) - 1)
    def _():
        for hh in range(heads):
            sout_ref[hh] = st_ref[hh].T


def _hgrn(bm3, logf3, k3, out_gain, n_heads, gain_col0, s0):
    B, S, W = logf3.shape
    chunk = _divisor_tile(S, HG_CHUNK)
    hb = _divisor_tile(n_heads, HG_HEADS_PER_STEP, align=1)
    wb = hb * HEAD_DIM
    ng = n_heads // hb
    assert gain_col0 % wb == 0

    def col(k):
        return pl.BlockSpec((None, chunk, wb), lambda b, h, c: (b, c, k * ng + h))

    in_specs = [col(0),
                pl.BlockSpec((None, chunk, wb), lambda b, h, c: (b, c, h)),
                pl.BlockSpec((None, chunk, wb), lambda b, h, c: (b, c, h)),
                col(1), col(2),
                pl.BlockSpec((1, wb), lambda b, h, c: (0, gain_col0 // wb + h))]
    args = [bm3, logf3, k3, bm3, bm3, out_gain]
    state_spec = pl.BlockSpec((None, hb, HEAD_DIM, HEAD_DIM), lambda b, h, c: (b, h, 0, 0))
    if s0 is not None:
        in_specs.append(state_spec)
        args.append(s0)
    return pl.pallas_call(
        functools.partial(_hgrn_kernel, chunk=chunk, heads=hb, has_state=s0 is not None),
        grid=(B, ng, S // chunk),
        in_specs=in_specs,
        out_specs=[pl.BlockSpec((None, chunk, wb), lambda b, h, c: (b, c, h)), state_spec],
        out_shape=[jax.ShapeDtypeStruct((B, S, W), BF16),
                   jax.ShapeDtypeStruct((B, n_heads, HEAD_DIM, HEAD_DIM), F32)],
        scratch_shapes=[pltpu.VMEM((hb, HEAD_DIM, HEAD_DIM), F32)],
        compiler_params=_cparams("parallel", "parallel", "arbitrary"),
        name="hgrn",
    )(*args)


def _place_queries(q_ref, q16_ref, n_heads, n_new):
    q16_ref[...] = jnp.zeros_like(q16_ref)
    for h in range(n_heads):
        q16_ref[h, 0:n_new, :] = q_ref[:, h * HEAD_DIM:(h + 1) * HEAD_DIM].astype(BF16)


def _page_heads(k_ref, v_ref, n_heads, page_rows):
    ks = [k_ref[pl.ds(h, page_rows, stride=n_heads), :].astype(BF16) for h in range(n_heads)]
    vs = [v_ref[pl.ds(h, page_rows, stride=n_heads), :].astype(BF16) for h in range(n_heads)]
    return ks, vs


def _new_heads(kvnew_ref, n_heads, page_rows):
    new = kvnew_ref[...]
    pad = jnp.zeros((page_rows - new.shape[0], new.shape[1]), F32)
    kv = jnp.concatenate([new, pad], axis=0).astype(BF16)
    ks = [kv[:, h * 2 * HEAD_DIM:h * 2 * HEAD_DIM + HEAD_DIM] for h in range(n_heads)]
    vs = [kv[:, h * 2 * HEAD_DIM + HEAD_DIM:(h + 1) * 2 * HEAD_DIM] for h in range(n_heads)]
    return ks, vs


def _head_scores(q16_ref, ks, n_new):
    return jnp.concatenate([_dot_nt(q16_ref[h], k)[:n_new] for h, k in enumerate(ks)], axis=0)


def _head_values(p, vs, n_new, q_rows):
    outs = []
    for h, v in enumerate(vs):
        ph = p[h * n_new:(h + 1) * n_new, :]
        if q_rows > n_new:
            ph = jnp.concatenate([ph, jnp.zeros((q_rows - n_new, ph.shape[1]), F32)], axis=0)
        outs.append(_dot(ph.astype(BF16), v)[:n_new])
    return jnp.concatenate(outs, axis=0)


def _store_heads(o, og_ref, o_ref, n_heads, n_new):
    for h in range(n_heads):
        sl = slice(h * HEAD_DIM, (h + 1) * HEAD_DIM)
        o_ref[:, sl] = _rms_rows(o[h * n_new:(h + 1) * n_new, :], og_ref[:, sl]).astype(BF16)


def _fox_sample_kernel(pt_ref, q_ref, kvnew_ref, e_ref, et_ref, *rest, n_heads, n_new, page_rows, pps):
    del pt_ref
    k_refs, v_refs, lft_refs = rest[:pps], rest[pps:2 * pps], rest[2 * pps:3 * pps]
    og_ref, o_ref, q16_ref, m_ref, l_ref, acc_ref, run_ref, ecol_ref = rest[3 * pps:]
    p = pl.program_id(1)
    rows = n_heads * n_new
    q_rows = q16_ref.shape[1]

    def update(s, v_chunks):
        m_old = m_ref[...]
        m_new = jnp.maximum(m_old, jnp.max(s, axis=1, keepdims=True))
        alpha = jnp.exp(m_old - m_new)
        pm = jnp.exp(s - m_new)
        l_ref[...] = alpha * l_ref[...] + jnp.sum(pm, axis=1, keepdims=True)
        acc = alpha * acc_ref[...]
        for c, vs in enumerate(v_chunks):
            acc = acc + _head_values(pm[:, c * page_rows:(c + 1) * page_rows], vs, n_new, q_rows)
        acc_ref[...] = acc
        m_ref[...] = m_new

    @pl.when(p == 0)
    def _():
        _place_queries(q_ref, q16_ref, n_heads, n_new)
        m_ref[...] = jnp.full_like(m_ref, -jnp.inf)
        l_ref[...] = jnp.zeros_like(l_ref)
        acc_ref[...] = jnp.zeros_like(acc_ref)
        run_ref[...] = jnp.zeros_like(run_ref)
        e = e_ref[...]
        ecol = jnp.concatenate([_pick_lane(e, h) for h in range(n_heads)], axis=0)
        ecol_ref[...] = ecol
        row_io = lax.broadcasted_iota(jnp.int32, (rows, page_rows), 0)
        col_io = lax.broadcasted_iota(jnp.int32, (rows, page_rows), 1)
        ks, vs = _new_heads(kvnew_ref, n_heads, page_rows)
        s = _head_scores(q16_ref, ks, n_new) + ecol - _repeat_rows(et_ref[...], n_new)
        update(jnp.where(col_io <= jnp.bitwise_and(row_io, n_new - 1), s, -jnp.inf), [vs])

    later = _later_matrix(page_rows)
    run = run_ref[...]
    ecol = ecol_ref[...]
    s_chunks, v_chunks = [], []
    for r in range(pps):
        ks, vs = _page_heads(k_refs[r], v_refs[r], n_heads, page_rows)
        lf = _repeat_rows(lft_refs[r][...], n_new)
        s_chunks.append(_head_scores(q16_ref, ks, n_new) + ecol + _dot2_lhs(lf, later) + run)
        v_chunks.append(vs)
        run = run + jnp.sum(lf, axis=1, keepdims=True)
    run_ref[...] = run
    update(jnp.concatenate(s_chunks, axis=1), v_chunks)

    @pl.when(p == pl.num_programs(1) - 1)
    def _():
        _store_heads(acc_ref[...] / l_ref[...], og_ref, o_ref, n_heads, n_new)


def _sb_sample_kernel(*refs, n_heads, n_new, page_rows, pps, first):
    if first:
        q_ref, kvnew_ref = refs[1:3]
        rest = refs[3:]
    else:
        q_ref, acc_in_ref, run_in_ref = refs[2:5]
        rest = refs[5:]
    k_refs, v_refs = rest[:pps], rest[pps:2 * pps]
    if first:
        acc_out_ref, run_out_ref, q16_ref, acc_ref, run_ref = rest[2 * pps:]
    else:
        og_ref, o_ref, q16_ref, acc_ref, run_ref = rest[2 * pps:]
    p = pl.program_id(1)
    rows = n_heads * n_new
    q_rows = q16_ref.shape[1]
    later = _later_matrix(page_rows)

    def step(ks, vs, run, acc, mask):
        z = _head_scores(q16_ref, ks, n_new)
        lsm = _log_sigmoid(-z)
        if mask is not None:
            lsm = jnp.where(mask, lsm, 0.0)
        a = jnp.exp(lsm + z + _dot2_lhs(lsm, later) + run)
        if mask is not None:
            a = jnp.where(mask, a, 0.0)
        return run + jnp.sum(lsm, axis=1, keepdims=True), acc + _head_values(a, vs, n_new, q_rows)

    @pl.when(p == 0)
    def _():
        _place_queries(q_ref, q16_ref, n_heads, n_new)
        if first:
            row_io = lax.broadcasted_iota(jnp.int32, (rows, page_rows), 0)
            col_io = lax.broadcasted_iota(jnp.int32, (rows, page_rows), 1)
            ks, vs = _new_heads(kvnew_ref, n_heads, page_rows)
            run, acc = step(ks, vs, jnp.zeros((rows, 1), F32), jnp.zeros((rows, HEAD_DIM), F32),
                            col_io < jnp.bitwise_and(row_io, n_new - 1))
            run_ref[...] = run
            acc_ref[...] = acc
        else:
            run_ref[...] = run_in_ref[...]
            acc_ref[...] = acc_in_ref[...]

    @pl.when(jnp.max(run_ref[...]) > EXP_UNDERFLOW)
    def _():
        run, acc = run_ref[...], acc_ref[...]
        for r in range(pps):
            ks, vs = _page_heads(k_refs[r], v_refs[r], n_heads, page_rows)
            run, acc = step(ks, vs, run, acc, None)
        run_ref[...] = run
        acc_ref[...] = acc

    @pl.when(p == pl.num_programs(1) - 1)
    def _():
        if first:
            acc_out_ref[...] = acc_ref[...]
            run_out_ref[...] = run_ref[...]
        else:
            _store_heads(acc_ref[...], og_ref, o_ref, n_heads, n_new)


def _page_spec(shape, n_pages, pps, layer, r, col=0, step0=0, gated=False):
    def index(b, p, pt, *flags):
        slot = n_pages - 1 - ((p + step0) * pps + r)
        if gated:
            slot = jnp.where(flags[0][b] > 0, slot, r)
        return (pt[b * n_pages + slot], layer, 0, col)
    return pl.BlockSpec((None, None) + shape, index)


def _kv_page_specs(rows, n_pages, pps, layer, step0=0, gated=False):
    return [_page_spec((rows, HEAD_DIM), n_pages, pps, layer, r, col, step0, gated)
            for col in (0, 1) for r in range(pps)]


def _decode_common(page_table, q3, cache_kv, n_heads):
    B, L, _ = q3.shape
    n_pages = page_table.shape[1]
    page_rows = cache_kv.shape[2] // n_heads
    pps = _divisor_tile(n_pages, PAGES_PER_STEP, align=1)
    assert L & (L - 1) == 0 and L % SUBLANES == 0 and L <= page_rows and page_rows == LANES
    q_rows = -(-L // BF16_ROWS) * BF16_ROWS
    return B, L, n_pages, page_rows, pps, q_rows


def _fox_sample(page_table, q3, kvnew4, cum3, cum_t, cache_kv, cache_lft, out_gain, layer, n_heads):
    B, L, n_pages, page_rows, pps, q_rows = _decode_common(page_table, q3, cache_kv, n_heads)
    rows = n_heads * L
    qw = n_heads * HEAD_DIM
    lf_shape = (n_heads, page_rows)
    grid_spec = pltpu.PrefetchScalarGridSpec(
        num_scalar_prefetch=1,
        grid=(B, n_pages // pps),
        in_specs=[pl.BlockSpec((None, L, qw), lambda b, p, pt: (b, 0, 0)),
                  pl.BlockSpec((None, None, L, 2 * qw), lambda b, p, pt: (b, layer, 0, 0)),
                  pl.BlockSpec((None, L, LANES), lambda b, p, pt: (b, 0, 0)),
                  pl.BlockSpec((None, n_heads, LANES), lambda b, p, pt: (b, 0, 0))]
        + _kv_page_specs(page_rows * n_heads, n_pages, pps, layer)
        + [_page_spec(lf_shape, n_pages, pps, layer, r) for r in range(pps)]
        + [pl.BlockSpec((1, qw), lambda b, p, pt: (0, 0))],
        out_specs=pl.BlockSpec((None, L, qw), lambda b, p, pt: (b, 0, 0)),
        scratch_shapes=[pltpu.VMEM((n_heads, q_rows, HEAD_DIM), BF16), pltpu.VMEM((rows, 1), F32),
                        pltpu.VMEM((rows, 1), F32), pltpu.VMEM((rows, HEAD_DIM), F32),
                        pltpu.VMEM((rows, 1), F32), pltpu.VMEM((rows, 1), F32)])
    return pl.pallas_call(
        functools.partial(_fox_sample_kernel, n_heads=n_heads, n_new=L, page_rows=page_rows, pps=pps),
        grid_spec=grid_spec,
        out_shape=jax.ShapeDtypeStruct((B, L, qw), BF16),
        compiler_params=_cparams("parallel", "arbitrary"),
        name="fox_sample",
    )(page_table.reshape(-1), q3, kvnew4, cum3, cum_t, *([cache_kv] * (2 * pps)),
      *([cache_lft] * pps), out_gain)


def _sb_sample(page_table, q3, kvnew4, cache_kv, out_gain, layer, n_heads):
    B, L, n_pages, page_rows, pps, q_rows = _decode_common(page_table, q3, cache_kv, n_heads)
    rows = n_heads * L
    qw = n_heads * HEAD_DIM
    n_steps = n_pages // pps
    assert n_steps >= 2
    kernel_args = dict(n_heads=n_heads, n_new=L, page_rows=page_rows, pps=pps)
    scratch = [pltpu.VMEM((n_heads, q_rows, HEAD_DIM), BF16), pltpu.VMEM((rows, HEAD_DIM), F32),
               pltpu.VMEM((rows, 1), F32)]
    q_spec = pl.BlockSpec((None, L, qw), lambda b, p, *_: (b, 0, 1))
    acc_spec = pl.BlockSpec((None, rows, HEAD_DIM), lambda b, p, *_: (b, 0, 0))
    run_spec = pl.BlockSpec((None, rows, 1), lambda b, p, *_: (b, 0, 0))
    pt = page_table.reshape(-1)

    acc, run = pl.pallas_call(
        functools.partial(_sb_sample_kernel, first=True, **kernel_args),
        grid_spec=pltpu.PrefetchScalarGridSpec(
            num_scalar_prefetch=1, grid=(B, 1),
            in_specs=[q_spec, pl.BlockSpec((None, None, L, 2 * qw), lambda b, p, pt: (b, layer, 0, 0))]
            + _kv_page_specs(page_rows * n_heads, n_pages, pps, layer),
            out_specs=[acc_spec, run_spec], scratch_shapes=scratch),
        out_shape=[jax.ShapeDtypeStruct((B, rows, HEAD_DIM), F32),
                   jax.ShapeDtypeStruct((B, rows, 1), F32)],
        compiler_params=_cparams("parallel", "arbitrary"),
        name="sb_sample_head",
    )(pt, q3, kvnew4, *([cache_kv] * (2 * pps)))

    alive = (jnp.max(run, axis=(1, 2)) > EXP_UNDERFLOW).astype(jnp.int32)
    return pl.pallas_call(
        functools.partial(_sb_sample_kernel, first=False, **kernel_args),
        grid_spec=pltpu.PrefetchScalarGridSpec(
            num_scalar_prefetch=2, grid=(B, n_steps - 1),
            in_specs=[q_spec, acc_spec, run_spec]
            + _kv_page_specs(page_rows * n_heads, n_pages, pps, layer, step0=1, gated=True)
            + [pl.BlockSpec((1, qw), lambda b, p, *_: (0, 1))],
            out_specs=pl.BlockSpec((None, L, qw), lambda b, p, *_: (b, 0, 0)),
            scratch_shapes=scratch),
        out_shape=jax.ShapeDtypeStruct((B, L, qw), BF16),
        compiler_params=_cparams("parallel", "arbitrary"),
        name="sb_sample_tail",
    )(pt, alive, q3, acc, run, *([cache_kv] * (2 * pps)), out_gain)


def _out_proj_kernel(x_ref, a_ref, b_ref, c_ref, wa_ref, wb_ref, wc_ref, o_ref):
    o_ref[...] = (x_ref[...] + _dot(a_ref[...], wa_ref[...]) + _dot(b_ref[...], wb_ref[...])
                  + _dot(c_ref[...], wc_ref[...]))


def _out_proj(x, oa, ob, oc, w, layer):
    T, D = x.shape
    tm, tn = _divisor_tile(T, ROW_TILE), _divisor_tile(D, COL_TILE)
    wa, wb, wc = oa.shape[1], ob.shape[1], oc.shape[1]
    assert wa % wb == 0 and (wa + wb) % wc == 0

    def act(a):
        return pl.BlockSpec((tm, a.shape[1]), lambda i, j: (i, 0))

    def band(rows, row_blk):
        return pl.BlockSpec((None, rows, tn), lambda i, j: (layer, row_blk, j))

    return pl.pallas_call(
        _out_proj_kernel,
        grid=(T // tm, D // tn),
        in_specs=[pl.BlockSpec((tm, tn), lambda i, j: (i, j)), act(oa), act(ob), act(oc),
                  band(wa, 0), band(wb, wa // wb), band(wc, (wa + wb) // wc)],
        out_specs=pl.BlockSpec((tm, tn), lambda i, j: (i, j)),
        out_shape=jax.ShapeDtypeStruct((T, D), F32),
        compiler_params=_cparams("parallel", "arbitrary"),
        name="out_proj",
    )(x, oa, ob, oc, w, w, w)


def _ple_kernel(x_ref, h_ref, wg_ref, p_ref, wp_ref, o_ref):
    gate = jax.nn.sigmoid(_dot(h_ref[...], wg_ref[...]))
    emb = _dot(p_ref[...].astype(BF16), wp_ref[...])
    o_ref[...] = x_ref[...] + gate * emb


def _ple(x, h, wg, p, wp, layer):
    T, D = x.shape
    tm, tn = _divisor_tile(T, ROW_TILE), _divisor_tile(D, COL_TILE)
    return pl.pallas_call(
        _ple_kernel,
        grid=(T // tm, D // tn),
        in_specs=[pl.BlockSpec((tm, tn), lambda i, j: (i, j)),
                  pl.BlockSpec((tm, D), lambda i, j: (i, 0)),
                  pl.BlockSpec((None, D, tn), lambda i, j: (layer, 0, j)),
                  pl.BlockSpec((tm, p.shape[1]), lambda i, j: (i, 0)),
                  pl.BlockSpec((None, p.shape[1], tn), lambda i, j: (layer, 0, j))],
        out_specs=pl.BlockSpec((tm, tn), lambda i, j: (i, j)),
        out_shape=jax.ShapeDtypeStruct((T, D), F32),
        compiler_params=_cparams("parallel", "arbitrary"),
        name="ple",
    )(x, h, wg, p, wp)


def _aligned_w_in(w_in, hf, hs, hg):
    fw, sw, gw = hf * HEAD_DIM, hs * HEAD_DIM, hg * HEAD_DIM
    gate0 = 3 * fw
    w = jnp.concatenate([w_in[..., :gate0], w_in[..., gate0 + hf:],
                         jnp.pad(w_in[..., gate0:gate0 + hf], ((0, 0), (0, 0), (0, LANES - hf)))],
                        axis=-1).astype(BF16)
    cols, o = {}, 0
    for name, size in (("fq", fw), ("fk", fw), ("fv", fw), ("sq", sw), ("sk", sw), ("sv", sw),
                       ("hq", gw), ("hf", gw), ("hi", gw), ("hg", gw), ("ff", LANES)):
        cols[name] = o
        o += size
    return w, cols


def kernel(x_prompt, x_sample, p_prompt, p_sample, cache_fox_kv, cache_fox_logf, cache_sb_kv, state_hgrn,
           page_table, ffn1_norm, ffn1_w1, ffn1_w3, ffn1_w2, mix_norm, w_in, fox_f_bias, fox_q_norm,
           fox_k_norm, sb_q_norm, sb_k_norm, hg_lower_bound, out_norm, w_out, ffn2_norm, ffn2_w1, ffn2_w3,
           ffn2_w2, ple_norm, w_ple_gate, w_ple_proj):
    depth = w_in.shape[0]
    B, S, D = x_prompt.shape
    Bs, L, _ = x_sample.shape
    hf, hs = fox_f_bias.shape[1], cache_sb_kv.shape[3]
    hg = state_hgrn.shape[2]
    assert hf == hs
    fw, sw, gw = hf * HEAD_DIM, hs * HEAD_DIM, hg * HEAD_DIM
    n_pool, _, page_rows, _, _ = cache_fox_kv.shape

    w_in_b, cols = _aligned_w_in(w_in, hf, hs, hg)
    ffn1 = (ffn1_w1.astype(BF16), ffn1_w3.astype(BF16), ffn1_w2.astype(BF16))
    ffn2 = (ffn2_w1.astype(BF16), ffn2_w3.astype(BF16), ffn2_w2.astype(BF16))
    w_out_b, w_gate_b, w_emb_b = w_out.astype(BF16), w_ple_gate.astype(BF16), w_ple_proj.astype(BF16)

    gamma = jnp.cumsum(jax.nn.softmax(hg_lower_bound.astype(F32), axis=0), axis=0)
    cache_fkv = cache_fox_kv.reshape(n_pool, depth, page_rows * hf, 2 * HEAD_DIM)
    cache_skv = cache_sb_kv.reshape(n_pool, depth, page_rows * hs, 2 * HEAD_DIM)
    cache_lft = jnp.swapaxes(cache_fox_logf.astype(F32), 2, 3)

    xp = x_prompt.reshape(B * S, D)
    xs = x_sample.reshape(Bs * L, D)
    hg_pad = _divisor_tile(max(L, HG_CHUNK), HG_CHUNK)
    pf_kv = psb_kv = sf_kv = ssb_kv = None
    logf_p, logf_s, state_p, state_s = [], [], [], []
    scale = HEAD_DIM ** -0.5

    for l in range(depth):
        lb = (gamma[l - 1] if l > 0 else jnp.zeros_like(gamma[0]))[None, :]
        out_gain = out_norm[l][None, :]
        q_gain = jnp.concatenate([jnp.tile(fox_q_norm[l] * scale, hf),
                                  jnp.tile(sb_q_norm[l] * scale, hs)])[None, :]
        f_bias = jnp.pad(fox_f_bias[l], (0, LANES - hf))[None, :]

        def pre_mix(x, batch, seq, fkv_prev, skv_prev):
            x, h = _ffn(x, ffn1_norm[l][None, :], *ffn1, l, next_gain=mix_norm[l][None, :])
            fkv = _proj_kv(h, w_in_b, l, cols["fk"], cols["fv"], fw, fox_k_norm[l][None, :],
                           (batch, depth, seq, fkv_prev))
            skv = _proj_kv(h, w_in_b, l, cols["sk"], cols["sv"], sw, sb_k_norm[l][None, :],
                           (batch, depth, seq, skv_prev))
            q = _proj_q(h, w_in_b, l, [cols["fq"], cols["sq"]], fw, q_gain).reshape(batch, seq, fw + sw)
            bm = _proj_plain(h, w_in_b, l, [cols["hq"], cols["hi"], cols["hg"]], gw)
            glogf, gk = _proj_hgrn_gate(h, w_in_b, l, cols["hf"], gw, lb)
            flogf, fcum = _proj_fox_gate(h, w_in_b, l, cols["ff"], f_bias, seq)
            return (x, fkv, skv, q, bm.reshape(batch, seq, 3 * gw), glogf.reshape(batch, seq, gw),
                    gk.reshape(batch, seq, gw), flogf.reshape(batch, seq, LANES),
                    fcum.reshape(batch, seq, LANES))

        def post_mix(x, o_fox, o_sb, o_hg, p_l):
            x = _out_proj(x, o_fox, o_sb, o_hg, w_out_b, l)
            x, h = _ffn(x, ffn2_norm[l][None, :], *ffn2, l, next_gain=ple_norm[l][None, :])
            return _ple(x, h, w_gate_b, p_l, w_emb_b, l)

        xp, pf_kv, psb_kv, q, bm, glogf, gk, flogf, fcum = pre_mix(xp, B, S, pf_kv, psb_kv)
        cum_t = jnp.swapaxes(fcum[:, :, :hf], 1, 2)
        o_fox = _fox_prompt(q, pf_kv, l, fcum, cum_t, out_gain, hf)
        o_sb = _sb_prompt(q, psb_kv, l, out_gain, hs, hf, hf)
        o_hg, st = _hgrn(bm, glogf, gk, out_gain, hg, fw + sw, None)
        xp = post_mix(xp, o_fox.reshape(B * S, fw), o_sb.reshape(B * S, sw), o_hg.reshape(B * S, gw),
                      p_prompt[l].reshape(B * S, -1))
        logf_p.append(flogf[:, :, :hf])
        state_p.append(st)

        xs, sf_kv, ssb_kv, q, bm, glogf, gk, flogf, fcum = pre_mix(xs, Bs, L, sf_kv, ssb_kv)
        cum_t = jnp.pad(jnp.swapaxes(fcum[:, :, :hf], 1, 2), ((0, 0), (0, 0), (0, LANES - L)))
        o_fox = _fox_sample(page_table, q, sf_kv, fcum, cum_t, cache_fkv, cache_lft, out_gain, l, hf)
        o_sb = _sb_sample(page_table, q, ssb_kv, cache_skv, out_gain, l, hs)

        def pad_tokens(z):
            return jnp.pad(z, ((0, 0), (0, hg_pad - L), (0, 0)))

        o_hg, st = _hgrn(pad_tokens(bm), pad_tokens(glogf), pad_tokens(gk), out_gain, hg, fw + sw,
                         state_hgrn[l].astype(F32))
        xs = post_mix(xs, o_fox.reshape(Bs * L, fw), o_sb.reshape(Bs * L, sw),
                      o_hg[:, :L].reshape(Bs * L, gw), p_sample[l].reshape(Bs * L, -1))
        logf_s.append(flogf[:, :, :hf])
        state_s.append(st)

    return (xp.reshape(B, S, D), xs.reshape(Bs, L, D),
            pf_kv.reshape(B, depth, S, hf, 2 * HEAD_DIM), jnp.stack(logf_p, axis=1),
            psb_kv.reshape(B, depth, S, hs, 2 * HEAD_DIM), jnp.stack(state_p, axis=0),
            sf_kv.reshape(Bs, depth, L, hf, 2 * HEAD_DIM), jnp.stack(logf_s, axis=1),
            ssb_kv.reshape(Bs, depth, L, hs, 2 * HEAD_DIM), jnp.stack(state_s, axis=0))
```

```python
import functools

import jax
import jax.numpy as jnp
from jax import lax
from jax.experimental import pallas as pl
from jax.experimental.pallas import tpu as pltpu

F32 = jnp.float32
BF16 = jnp.bfloat16

HEAD_DIM = 128
EPS = 1e-6
LANES = 128
SUBLANES = 8
BF16_ROWS = 16
VMEM_LIMIT_BYTES = 56 << 20
VMEM_LIMIT_FFN_BYTES = 60 << 20
ROW_TILE = 512
COL_TILE = 1024
FF_TILE = 256
GATE_CHUNK = 256
ATTN_BLOCK = 256
ATTN_HEADS_PER_STEP = 8
HG_CHUNK = 128
HG_HEADS_PER_STEP = 4
PAGES_PER_STEP = 16
EXP_UNDERFLOW = -104.0


def _cparams(*sem, vmem=VMEM_LIMIT_BYTES):
    return pltpu.CompilerParams(dimension_semantics=sem, vmem_limit_bytes=vmem)


def _divisor_tile(n, pref, align=LANES):
    if n <= pref:
        return n
    for t in range(pref - pref % align, 0, -align):
        if n % t == 0:
            return t
    raise ValueError((n, pref, align))


def _dot(a, b):
    return jnp.dot(a, b, preferred_element_type=F32)


def _dot_nt(a, b):
    return lax.dot_general(a, b, (((1,), (1,)), ((), ())), preferred_element_type=F32)


def _dot_tn(a, b):
    return lax.dot_general(a, b, (((0,), (0,)), ((), ())), preferred_element_type=F32)


def _dot3_rhs(m_bf16, x):
    hi = x.astype(BF16)
    r1 = x - hi.astype(F32)
    mid = r1.astype(BF16)
    lo = (r1 - mid.astype(F32)).astype(BF16)
    return _dot(m_bf16, hi) + _dot(m_bf16, mid) + _dot(m_bf16, lo)


def _dot2_lhs(x, m_bf16):
    hi = x.astype(BF16)
    lo = (x - hi.astype(F32)).astype(BF16)
    return _dot(hi, m_bf16) + _dot(lo, m_bf16)


def _log_sigmoid(x):
    return jnp.minimum(x, 0.0) - jnp.log1p(jnp.exp(-jnp.abs(x)))


def _rms_rows(x, gain):
    return x * lax.rsqrt(jnp.mean(x * x, axis=-1, keepdims=True) + EPS) * gain


def _ones_where(cond):
    return jnp.where(cond, 1.0, 0.0).astype(BF16)


def _later_matrix(n):
    j_io = lax.broadcasted_iota(jnp.int32, (n, n), 0)
    s_io = lax.broadcasted_iota(jnp.int32, (n, n), 1)
    return _ones_where(j_io > s_io)


def _pick_lane(x, idx):
    lane = lax.broadcasted_iota(jnp.int32, x.shape, 1)
    return jnp.sum(jnp.where(lane == idx, x, 0.0), axis=1, keepdims=True)


def _pick_row(x, idx):
    row = lax.broadcasted_iota(jnp.int32, x.shape, 0)
    return jnp.sum(jnp.where(row == idx, x, 0.0), axis=0, keepdims=True)


def _repeat_rows(x, reps):
    return jnp.concatenate(
        [jnp.broadcast_to(x[h:h + 1, :], (reps, x.shape[1])) for h in range(x.shape[0])], axis=0)


def _ffn_kernel(*refs, emit_norm):
    if emit_norm:
        x_ref, g_ref, w1_ref, w3_ref, w2_ref, g2_ref, o_ref, h2_ref, h_ref = refs
    else:
        x_ref, g_ref, w1_ref, w3_ref, w2_ref, o_ref, h_ref = refs

    @pl.when(pl.program_id(1) == 0)
    def _():
        x = x_ref[...]
        h_ref[...] = _rms_rows(x, g_ref[...]).astype(BF16)
        o_ref[...] = x

    h = h_ref[...]
    a = _dot(h, w1_ref[...])
    b = _dot(h, w3_ref[...])
    gated = (a * jax.nn.sigmoid(a) * b * 0.5).astype(BF16)
    o_ref[...] += _dot(gated, w2_ref[...])

    if emit_norm:
        @pl.when(pl.program_id(1) == pl.num_programs(1) - 1)
        def _():
            h2_ref[...] = _rms_rows(o_ref[...], g2_ref[...]).astype(BF16)


def _ffn(x, gain, w1, w3, w2, layer, next_gain=None):
    T, D = x.shape
    F = w1.shape[2]
    tm = _divisor_tile(T, ROW_TILE)
    tf = _divisor_tile(F, FF_TILE)
    emit = next_gain is not None
    row = pl.BlockSpec((1, D), lambda i, j: (0, 0))
    tile = pl.BlockSpec((tm, D), lambda i, j: (i, 0))
    in_specs = [pl.BlockSpec((tm, D), lambda i, j: (i, 0), pipeline_mode=pl.Buffered(1)), row,
                pl.BlockSpec((None, D, tf), lambda i, j: (layer, 0, j)),
                pl.BlockSpec((None, D, tf), lambda i, j: (layer, 0, j)),
                pl.BlockSpec((None, tf, D), lambda i, j: (layer, j, 0))]
    args = [x, gain, w1, w3, w2]
    out_specs, out_shape = tile, jax.ShapeDtypeStruct((T, D), F32)
    if emit:
        in_specs.append(row)
        args.append(next_gain)
        out_specs = [tile, tile]
        out_shape = [out_shape, jax.ShapeDtypeStruct((T, D), BF16)]
    return pl.pallas_call(
        functools.partial(_ffn_kernel, emit_norm=emit),
        grid=(T // tm, F // tf),
        in_specs=in_specs, out_specs=out_specs, out_shape=out_shape,
        scratch_shapes=[pltpu.VMEM((tm, D), BF16)],
        compiler_params=_cparams("parallel", "arbitrary", vmem=VMEM_LIMIT_FFN_BYTES),
        name="ffn",
    )(*args)


def _proj_q_kernel(h_ref, w_ref, hgain_ref, o_ref):
    z = _dot(h_ref[...], w_ref[...])
    for c in range(z.shape[1] // HEAD_DIM):
        sl = slice(c * HEAD_DIM, (c + 1) * HEAD_DIM)
        o_ref[:, sl] = _rms_rows(z[:, sl], hgain_ref[:, sl])


def _proj_kv_kernel(*refs):
    h_ref, wk_ref, wv_ref, kgain_ref = refs[:4]
    o_ref = refs[-1]
    h = h_ref[...]
    zk = _dot(h, wk_ref[...])
    zv = _dot(h, wv_ref[...])
    lead = o_ref.shape[:-1]
    for c in range(zk.shape[1] // HEAD_DIM):
        sl = slice(c * HEAD_DIM, (c + 1) * HEAD_DIM)
        k_out = slice(2 * c * HEAD_DIM, (2 * c + 1) * HEAD_DIM)
        v_out = slice((2 * c + 1) * HEAD_DIM, (2 * c + 2) * HEAD_DIM)
        o_ref[..., k_out] = _rms_rows(zk[:, sl], kgain_ref[...]).reshape(*lead, HEAD_DIM)
        o_ref[..., v_out] = zv[:, sl].reshape(*lead, HEAD_DIM)


def _proj_plain_kernel(h_ref, w_ref, o_ref):
    o_ref[...] = _dot(h_ref[...], w_ref[...])


def _proj_hgrn_gate_kernel(h_ref, w_ref, lb_ref, logf_ref, k_ref):
    h = h_ref[...]
    for c in range(0, w_ref.shape[1], GATE_CHUNK):
        sl = slice(c, c + GATE_CHUNK)
        z = _dot(h, w_ref[:, sl])
        lb = lb_ref[:, sl]
        e = jnp.exp(-jnp.abs(z))
        r = 1.0 / (1.0 + e)
        a = jnp.log1p(-lb) + jnp.minimum(z, 0.0) - jnp.log1p(e)
        b = jnp.log(lb)
        logf_ref[:, sl] = jnp.maximum(a, b) + jnp.log1p(jnp.exp(-jnp.abs(a - b)))
        k_ref[:, sl] = (1.0 - lb) * jnp.where(z >= 0.0, e * r, r)


def _proj_fox_gate_kernel(h_ref, w_ref, bias_ref, logf_ref, cum_ref, carry_ref, *, seq, tm):
    i = pl.program_id(0)
    logf = _log_sigmoid(_dot(h_ref[...], w_ref[...]) + bias_ref[...])
    logf_ref[...] = logf
    t = lax.broadcasted_iota(jnp.int32, (tm, tm), 0)
    s = lax.broadcasted_iota(jnp.int32, (tm, tm), 1)
    if seq >= tm:
        @pl.when(i % (seq // tm) == 0)
        def _():
            carry_ref[...] = jnp.zeros_like(carry_ref)

        cum = _dot3_rhs(_ones_where(s <= t), logf) + carry_ref[...]
        carry_ref[...] = cum[tm - 1:tm, :]
    else:
        shift = seq.bit_length() - 1
        same = lax.shift_right_logical(t, shift) == lax.shift_right_logical(s, shift)
        cum = _dot3_rhs(_ones_where(same & (s <= t)), logf)
    cum_ref[...] = cum


def _proj_in_specs(tm, D, tn, layer, seg_cols, seg_width):
    per = seg_width // tn
    starts = [c // tn for c in seg_cols]
    assert seg_width % tn == 0 and all(c % tn == 0 for c in seg_cols)

    def w_index(i, j):
        blk = starts[-1]
        for k in range(len(starts) - 2, -1, -1):
            blk = jnp.where(j // per == k, starts[k], blk)
        return (layer, 0, blk + j % per)

    return [pl.BlockSpec((tm, D), lambda i, j: (i, 0)), pl.BlockSpec((None, D, tn), w_index)]


def _proj_tiles(h, seg_cols, seg_width):
    T, D = h.shape
    N = len(seg_cols) * seg_width
    return T, D, N, _divisor_tile(T, ROW_TILE), _divisor_tile(seg_width, COL_TILE)


def _proj_q(h, w, layer, seg_cols, seg_width, hgain):
    T, D, N, tm, tn = _proj_tiles(h, seg_cols, seg_width)
    return pl.pallas_call(
        _proj_q_kernel,
        grid=(T // tm, N // tn),
        in_specs=_proj_in_specs(tm, D, tn, layer, seg_cols, seg_width)
        + [pl.BlockSpec((1, tn), lambda i, j: (0, j))],
        out_specs=pl.BlockSpec((tm, tn), lambda i, j: (i, j)),
        out_shape=jax.ShapeDtypeStruct((T, N), F32),
        compiler_params=_cparams("parallel", "arbitrary"),
        name="proj_q",
    )(h, w, hgain)


def _proj_kv(h, w, layer, k_col, v_col, width, kgain, dest):
    T, D = h.shape
    tm, tk = _divisor_tile(T, ROW_TILE), _divisor_tile(width, COL_TILE // 2)
    assert k_col % tk == 0 and v_col % tk == 0
    batch, depth, rows, prev = dest
    assert batch * rows == T
    in_specs = [pl.BlockSpec((tm, D), lambda i, j: (i, 0)),
                pl.BlockSpec((None, D, tk), lambda i, j: (layer, 0, k_col // tk + j)),
                pl.BlockSpec((None, D, tk), lambda i, j: (layer, 0, v_col // tk + j)),
                pl.BlockSpec((1, HEAD_DIM), lambda i, j: (0, 0))]
    args = [h, w, w, kgain]
    if rows >= tm:
        per = rows // tm
        out_spec = pl.BlockSpec((1, None, tm, 2 * tk), lambda i, j: (i // per, layer, i % per, j))
    else:
        assert tm % rows == 0 and rows % SUBLANES == 0
        out_spec = pl.BlockSpec((tm // rows, None, rows, 2 * tk), lambda i, j: (i, layer, 0, j))
    aliases = {}
    if prev is not None:
        in_specs.append(pl.BlockSpec(memory_space=pl.ANY))
        args.append(prev)
        aliases = {len(args) - 1: 0}
    return pl.pallas_call(
        _proj_kv_kernel,
        grid=(T // tm, width // tk),
        in_specs=in_specs, out_specs=out_spec,
        out_shape=jax.ShapeDtypeStruct((batch, depth, rows, 2 * width), F32),
        input_output_aliases=aliases,
        compiler_params=_cparams("parallel", "arbitrary"),
        name="proj_kv",
    )(*args)


def _proj_plain(h, w, layer, seg_cols, seg_width):
    T, D, N, tm, tn = _proj_tiles(h, seg_cols, seg_width)
    return pl.pallas_call(
        _proj_plain_kernel,
        grid=(T // tm, N // tn),
        in_specs=_proj_in_specs(tm, D, tn, layer, seg_cols, seg_width),
        out_specs=pl.BlockSpec((tm, tn), lambda i, j: (i, j)),
        out_shape=jax.ShapeDtypeStruct((T, N), F32),
        compiler_params=_cparams("parallel", "arbitrary"),
        name="proj_plain",
    )(h, w)


def _proj_hgrn_gate(h, w, layer, col, width, lb):
    T, D, N, tm, tn = _proj_tiles(h, [col], width)
    out = pl.BlockSpec((tm, tn), lambda i, j: (i, j))
    return pl.pallas_call(
        _proj_hgrn_gate_kernel,
        grid=(T // tm, N // tn),
        in_specs=_proj_in_specs(tm, D, tn, layer, [col], width)
        + [pl.BlockSpec((1, tn), lambda i, j: (0, j))],
        out_specs=[out, out],
        out_shape=[jax.ShapeDtypeStruct((T, N), F32)] * 2,
        compiler_params=_cparams("parallel", "arbitrary"),
        name="proj_hgrn_gate",
    )(h, w, lb)


def _proj_fox_gate(h, w, layer, col, bias, seq):
    T, D = h.shape
    tm = _divisor_tile(T, ROW_TILE)
    assert (seq % tm == 0) or (tm % seq == 0 and seq & (seq - 1) == 0)
    assert col % LANES == 0
    out = pl.BlockSpec((tm, LANES), lambda i: (i, 0))
    return pl.pallas_call(
        functools.partial(_proj_fox_gate_kernel, seq=seq, tm=tm),
        grid=(T // tm,),
        in_specs=[pl.BlockSpec((tm, D), lambda i: (i, 0)),
                  pl.BlockSpec((None, D, LANES), lambda i: (layer, 0, col // LANES)),
                  pl.BlockSpec((1, LANES), lambda i: (0, 0))],
        out_specs=[out, out],
        out_shape=[jax.ShapeDtypeStruct((T, LANES), F32)] * 2,
        scratch_shapes=[pltpu.VMEM((1, LANES), F32)],
        compiler_params=_cparams("arbitrary"),
        name="proj_fox_gate",
    )(h, w, bias)


def _kv_block(kv_ref, off, blk, hh):
    k = kv_ref[pl.ds(off, blk), hh * 2 * HEAD_DIM:hh * 2 * HEAD_DIM + HEAD_DIM].astype(BF16)
    v = kv_ref[pl.ds(off, blk), hh * 2 * HEAD_DIM + HEAD_DIM:(hh + 1) * 2 * HEAD_DIM].astype(BF16)
    return k, v


def _fox_prompt_kernel(q_ref, kv_ref, ccol_ref, crow_ref, og_ref, o_ref, *, blk, hp):
    g = pl.program_id(1)
    i = pl.program_id(2)
    heads = range(hp)
    qs = [q_ref[:, hh * HEAD_DIM:(hh + 1) * HEAD_DIM].astype(BF16) for hh in heads]
    ccols = [_pick_lane(ccol_ref[...], g * hp + hh) for hh in heads]
    t_io = lax.broadcasted_iota(jnp.int32, (blk, blk), 0)
    s_io = lax.broadcasted_iota(jnp.int32, (blk, blk), 1)

    def block(j, carry, diagonal):
        off = pl.multiple_of(j * blk, blk)
        crows = crow_ref[:, pl.ds(off, blk)]
        out = []
        for hh in heads:
            m, l, acc = carry[hh]
            k, v = _kv_block(kv_ref, off, blk, hh)
            s = _dot_nt(qs[hh], k) + ccols[hh] - _pick_row(crows, g * hp + hh)
            if diagonal:
                s = jnp.where(s_io <= t_io, s, -jnp.inf)
            m_new = jnp.maximum(m, jnp.max(s, axis=1, keepdims=True))
            alpha = jnp.exp(m - m_new)
            p = jnp.exp(s - m_new)
            l = alpha * l + jnp.sum(p, axis=1, keepdims=True)
            acc = alpha * acc + _dot(p.astype(BF16), v)
            out.append((m_new, l, acc))
        return tuple(out)

    init = tuple((jnp.full((blk, 1), -jnp.inf, F32), jnp.zeros((blk, 1), F32),
                  jnp.zeros((blk, HEAD_DIM), F32)) for _ in heads)
    carry = lax.fori_loop(0, i, lambda j, c: block(j, c, False), init)
    carry = block(i, carry, True)
    for hh in heads:
        _, l, acc = carry[hh]
        sl = slice(hh * HEAD_DIM, (hh + 1) * HEAD_DIM)
        o_ref[:, sl] = _rms_rows(acc / l, og_ref[:, sl]).astype(BF16)


def _sb_prompt_kernel(q_ref, kv_ref, og_ref, o_ref, *, blk, hp):
    i = pl.program_id(2)
    heads = range(hp)
    qs = [q_ref[:, hh * HEAD_DIM:(hh + 1) * HEAD_DIM].astype(BF16) for hh in heads]
    later = _later_matrix(blk)
    t_io = lax.broadcasted_iota(jnp.int32, (blk, blk), 0)
    s_io = lax.broadcasted_iota(jnp.int32, (blk, blk), 1)
    strict = s_io < t_io

    def block(j, carry, diagonal):
        off = pl.multiple_of(j * blk, blk)
        out = []
        for hh in heads:
            run, acc = carry[hh]
            k, v = _kv_block(kv_ref, off, blk, hh)
            z = _dot_nt(qs[hh], k)
            lsm = _log_sigmoid(-z)
            if diagonal:
                lsm = jnp.where(strict, lsm, 0.0)
            a = jnp.exp(lsm + z + _dot2_lhs(lsm, later) + run)
            if diagonal:
                a = jnp.where(strict, a, 0.0)
            out.append((run + jnp.sum(lsm, axis=1, keepdims=True), acc + _dot(a.astype(BF16), v)))
        return tuple(out)

    def alive(carry):
        top = jnp.max(carry[0][0])
        for hh in range(1, hp):
            top = jnp.maximum(top, jnp.max(carry[hh][0]))
        return (top > EXP_UNDERFLOW).astype(jnp.int32)

    init = tuple((jnp.zeros((blk, 1), F32), jnp.zeros((blk, HEAD_DIM), F32)) for _ in heads)
    carry = block(i, init, True)

    def body(state):
        j, _, c = state
        c = block(j, c, False)
        return j - 1, alive(c), c

    _, _, carry = lax.while_loop(lambda st: jnp.logical_and(st[0] >= 0, st[1] > 0), body,
                                 (i - 1, alive(carry), carry))
    for hh in heads:
        sl = slice(hh * HEAD_DIM, (hh + 1) * HEAD_DIM)
        o_ref[:, sl] = _rms_rows(carry[hh][1], og_ref[:, sl]).astype(BF16)


def _attn_prompt_specs(S, blk, hp, layer, q_blk0, gain_blk0):
    qw, kvw = hp * HEAD_DIM, hp * 2 * HEAD_DIM
    q_spec = pl.BlockSpec((None, blk, qw), lambda b, g, i: (b, i, q_blk0 + g))
    kv_spec = pl.BlockSpec((None, None, S, kvw), lambda b, g, i: (b, layer, 0, g))
    gain_spec = pl.BlockSpec((1, qw), lambda b, g, i: (0, gain_blk0 + g))
    out_spec = pl.BlockSpec((None, blk, qw), lambda b, g, i: (b, i, g))
    return q_spec, kv_spec, gain_spec, out_spec


def _fox_prompt(q3, kv4, layer, cum3, cum_t, out_gain, n_heads):
    B, S, _ = q3.shape
    blk = _divisor_tile(S, ATTN_BLOCK)
    hp = _divisor_tile(n_heads, ATTN_HEADS_PER_STEP, align=1)
    q_spec, kv_spec, gain_spec, out_spec = _attn_prompt_specs(S, blk, hp, layer, 0, 0)
    return pl.pallas_call(
        functools.partial(_fox_prompt_kernel, blk=blk, hp=hp),
        grid=(B, n_heads // hp, S // blk),
        in_specs=[q_spec, kv_spec,
                  pl.BlockSpec((None, blk, LANES), lambda b, g, i: (b, i, 0)),
                  pl.BlockSpec((None, n_heads, S), lambda b, g, i: (b, 0, 0)),
                  gain_spec],
        out_specs=out_spec,
        out_shape=jax.ShapeDtypeStruct((B, S, n_heads * HEAD_DIM), BF16),
        compiler_params=_cparams("parallel", "parallel", "arbitrary"),
        name="fox_prompt",
    )(q3, kv4, cum3, cum_t, out_gain)


def _sb_prompt(q3, kv4, layer, out_gain, n_heads, q_head0, gain_head0):
    B, S, _ = q3.shape
    blk = _divisor_tile(S, ATTN_BLOCK)
    hp = _divisor_tile(n_heads, ATTN_HEADS_PER_STEP, align=1)
    assert q_head0 % hp == 0 and gain_head0 % hp == 0
    q_spec, kv_spec, gain_spec, out_spec = _attn_prompt_specs(S, blk, hp, layer, q_head0 // hp,
                                                              gain_head0 // hp)
    return pl.pallas_call(
        functools.partial(_sb_prompt_kernel, blk=blk, hp=hp),
        grid=(B, n_heads // hp, S // blk),
        in_specs=[q_spec, kv_spec, gain_spec],
        out_specs=out_spec,
        out_shape=jax.ShapeDtypeStruct((B, S, n_heads * HEAD_DIM), BF16),
        compiler_params=_cparams("parallel", "parallel", "arbitrary"),
        name="sb_prompt",
    )(q3, kv4, out_gain)


def _hgrn_kernel(*refs, chunk, heads, has_state):
    if has_state:
        q_ref, g_ref, k_ref, i_ref, gate_ref, og_ref, s0_ref, o_ref, sout_ref, st_ref = refs
    else:
        q_ref, g_ref, k_ref, i_ref, gate_ref, og_ref, o_ref, sout_ref, st_ref = refs
    c = pl.program_id(2)
    width = heads * HEAD_DIM
    groups = chunk // SUBLANES

    @pl.when(c == 0)
    def _():
        for hh in range(heads):
            if has_state:
                st_ref[hh] = s0_ref[hh].T
            else:
                st_ref[hh] = jnp.zeros((HEAD_DIM, HEAD_DIM), F32)

    q = q_ref[...]
    g = g_ref[...]
    kk = k_ref[...]
    iv = i_ref[...].astype(BF16)

    rig = lax.broadcasted_iota(jnp.int32, (SUBLANES, width), 0)
    zero_row = jnp.zeros((1, width), F32)
    gg, bg, carry = [], [], zero_row
    for gi in range(groups):
        rows = g[gi * SUBLANES:(gi + 1) * SUBLANES, :]
        local = jnp.zeros((SUBLANES, width), F32)
        for j in range(SUBLANES):
            local = local + jnp.where(rig >= j, rows[j:j + 1, :], 0.0)
        b_rows = local + carry
        carry = b_rows[SUBLANES - 1:SUBLANES, :]
        gg.append(rows)
        bg.append(b_rows)
    b_last = carry

    def b_at(gi, j):
        return zero_row if gi < 0 else bg[gi][j:j + 1, :]

    def pick(values):
        m = SUBLANES // len(values)
        out = values[-1]
        for k in range(len(values) - 2, -1, -1):
            out = jnp.where(rig < (k + 1) * m, values[k], out)
        return out

    levels = []
    m = chunk // 2
    while m >= 1:
        levels.append(m)
        m //= 2

    q_lvl, k_lvl = {}, {}
    for m in levels:
        q_rows, k_rows = [], []
        for gi in range(groups):
            qg = q[gi * SUBLANES:(gi + 1) * SUBLANES, :]
            kg = kk[gi * SUBLANES:(gi + 1) * SUBLANES, :]
            if m == 1:
                u, v = gg[gi], None
            elif m >= SUBLANES:
                r = m // SUBLANES
                first = r * (gi // r)
                if (gi // r) % 2 == 1:
                    q_rows.append(qg * jnp.exp(bg[gi] - b_at(first - 1, SUBLANES - 1)))
                    k_rows.append(jnp.zeros_like(kg))
                else:
                    q_rows.append(jnp.zeros_like(qg))
                    k_rows.append(kg * jnp.exp(b_at(first + r - 1, SUBLANES - 1) - bg[gi]))
                continue
            else:
                n = SUBLANES // m
                prev = pick([b_at(gi - 1, SUBLANES - 1)] + [b_at(gi, k * m - 1) for k in range(1, n)])
                last = pick([b_at(gi, k * m + m - 1) for k in range(n)])
                u = bg[gi] - prev
                v = last - bg[gi]
            q_rows.append(qg * jnp.exp(u))
            k_rows.append(kg if v is None else kg * jnp.exp(v))
        q_lvl[m] = jnp.concatenate(q_rows, axis=0).astype(BF16)
        k_lvl[m] = jnp.concatenate(k_rows, axis=0).astype(BF16)

    b_all = jnp.concatenate(bg, axis=0)
    q_state = (q * jnp.exp(b_all)).astype(BF16)
    k_state = (kk * jnp.exp(b_last - b_all)).astype(BF16)
    decay = jnp.exp(b_last)
    q_bf = q.astype(BF16)
    k_bf = kk.astype(BF16)

    t_io = lax.broadcasted_iota(jnp.int32, (chunk, chunk), 0)
    s_io = lax.broadcasted_iota(jnp.int32, (chunk, chunk), 1)
    masks = {}
    for m in levels:
        sh = (2 * m).bit_length() - 1
        same = lax.shift_right_logical(t_io, sh) == lax.shift_right_logical(s_io, sh)
        upper = jnp.bitwise_and(t_io, 2 * m - 1) >= m
        lower = jnp.bitwise_and(s_io, 2 * m - 1) < m
        masks[m] = same & upper & lower

    for hh in range(heads):
        sl = slice(hh * HEAD_DIM, (hh + 1) * HEAD_DIM)
        w = jnp.where(t_io == s_io, _dot_nt(q_bf[:, sl], k_bf[:, sl]), 0.0)
        for m in levels:
            w = w + jnp.where(masks[m], _dot_nt(q_lvl[m][:, sl], k_lvl[m][:, sl]), 0.0)
        st = st_ref[hh]
        o = _dot(w.astype(BF16), iv[:, sl]) + _dot_nt(q_state[:, sl], st.astype(BF16))
        st_new = st * decay[:, sl] + _dot_tn(iv[:, sl], k_state[:, sl])
        st_ref[hh] = st_new
        gate = gate_ref[:, sl]
        o_ref[:, sl] = (_rms_rows(o, og_ref[:, sl]) * (gate * jax.nn.sigmoid(gate))).astype(BF16)

    @pl.when(c == pl.num_programs(2) - 1)
    def _():
        for hh in range(heads):
            sout_ref[hh] = st_ref[hh].T


def _hgrn(bm3, logf3, k3, out_gain, n_heads, gain_col0, s0):
    B, S, W = logf3.shape
    chunk = _divisor_tile(S, HG_CHUNK)
    hb = _divisor_tile(n_heads, HG_HEADS_PER_STEP, align=1)
    wb = hb * HEAD_DIM
    ng = n_heads // hb
    assert gain_col0 % wb == 0

    def col(k):
        return pl.BlockSpec((None, chunk, wb), lambda b, h, c: (b, c, k * ng + h))

    in_specs = [col(0),
                pl.BlockSpec((None, chunk, wb), lambda b, h, c: (b, c, h)),
                pl.BlockSpec((None, chunk, wb), lambda b, h, c: (b, c, h)),
                col(1), col(2),
                pl.BlockSpec((1, wb), lambda b, h, c: (0, gain_col0 // wb + h))]
    args = [bm3, logf3, k3, bm3, bm3, out_gain]
    state_spec = pl.BlockSpec((None, hb, HEAD_DIM, HEAD_DIM), lambda b, h, c: (b, h, 0, 0))
    if s0 is not None:
        in_specs.append(state_spec)
        args.append(s0)
    return pl.pallas_call(
        functools.partial(_hgrn_kernel, chunk=chunk, heads=hb, has_state=s0 is not None),
        grid=(B, ng, S // chunk),
        in_specs=in_specs,
        out_specs=[pl.BlockSpec((None, chunk, wb), lambda b, h, c: (b, c, h)), state_spec],
        out_shape=[jax.ShapeDtypeStruct((B, S, W), BF16),
                   jax.ShapeDtypeStruct((B, n_heads, HEAD_DIM, HEAD_DIM), F32)],
        scratch_shapes=[pltpu.VMEM((hb, HEAD_DIM, HEAD_DIM), F32)],
        compiler_params=_cparams("parallel", "parallel", "arbitrary"),
        name="hgrn",
    )(*args)


def _place_queries(q_ref, q16_ref, n_heads, n_new):
    q16_ref[...] = jnp.zeros_like(q16_ref)
    for h in range(n_heads):
        q16_ref[h, 0:n_new, :] = q_ref[:, h * HEAD_DIM:(h + 1) * HEAD_DIM].astype(BF16)


def _page_heads(k_ref, v_ref, n_heads, page_rows):
    ks = [k_ref[pl.ds(h, page_rows, stride=n_heads), :].astype(BF16) for h in range(n_heads)]
    vs = [v_ref[pl.ds(h, page_rows, stride=n_heads), :].astype(BF16) for h in range(n_heads)]
    return ks, vs


def _new_heads(kvnew_ref, n_heads, page_rows):
    new = kvnew_ref[...]
    pad = jnp.zeros((page_rows - new.shape[0], new.shape[1]), F32)
    kv = jnp.concatenate([new, pad], axis=0).astype(BF16)
    ks = [kv[:, h * 2 * HEAD_DIM:h * 2 * HEAD_DIM + HEAD_DIM] for h in range(n_heads)]
    vs = [kv[:, h * 2 * HEAD_DIM + HEAD_DIM:(h + 1) * 2 * HEAD_DIM] for h in range(n_heads)]
    return ks, vs


def _head_scores(q16_ref, ks, n_new):
    return jnp.concatenate([_dot_nt(q16_ref[h], k)[:n_new] for h, k in enumerate(ks)], axis=0)


def _head_values(p, vs, n_new, q_rows):
    outs = []
    for h, v in enumerate(vs):
        ph = p[h * n_new:(h + 1) * n_new, :]
        if q_rows > n_new:
            ph = jnp.concatenate([ph, jnp.zeros((q_rows - n_new, ph.shape[1]), F32)], axis=0)
        outs.append(_dot(ph.astype(BF16), v)[:n_new])
    return jnp.concatenate(outs, axis=0)


def _store_heads(o, og_ref, o_ref, n_heads, n_new):
    for h in range(n_heads):
        sl = slice(h * HEAD_DIM, (h + 1) * HEAD_DIM)
        o_ref[:, sl] = _rms_rows(o[h * n_new:(h + 1) * n_new, :], og_ref[:, sl]).astype(BF16)


def _fox_sample_kernel(pt_ref, q_ref, kvnew_ref, e_ref, et_ref, *rest, n_heads, n_new, page_rows, pps):
    del pt_ref
    k_refs, v_refs, lft_refs = rest[:pps], rest[pps:2 * pps], rest[2 * pps:3 * pps]
    og_ref, o_ref, q16_ref, m_ref, l_ref, acc_ref, run_ref, ecol_ref = rest[3 * pps:]
    p = pl.program_id(1)
    rows = n_heads * n_new
    q_rows = q16_ref.shape[1]

    def update(s, v_chunks):
        m_old = m_ref[...]
        m_new = jnp.maximum(m_old, jnp.max(s, axis=1, keepdims=True))
        alpha = jnp.exp(m_old - m_new)
        pm = jnp.exp(s - m_new)
        l_ref[...] = alpha * l_ref[...] + jnp.sum(pm, axis=1, keepdims=True)
        acc = alpha * acc_ref[...]
        for c, vs in enumerate(v_chunks):
            acc = acc + _head_values(pm[:, c * page_rows:(c + 1) * page_rows], vs, n_new, q_rows)
        acc_ref[...] = acc
        m_ref[...] = m_new

    @pl.when(p == 0)
    def _():
        _place_queries(q_ref, q16_ref, n_heads, n_new)
        m_ref[...] = jnp.full_like(m_ref, -jnp.inf)
        l_ref[...] = jnp.zeros_like(l_ref)
        acc_ref[...] = jnp.zeros_like(acc_ref)
        run_ref[...] = jnp.zeros_like(run_ref)
        e = e_ref[...]
        ecol = jnp.concatenate([_pick_lane(e, h) for h in range(n_heads)], axis=0)
        ecol_ref[...] = ecol
        row_io = lax.broadcasted_iota(jnp.int32, (rows, page_rows), 0)
        col_io = lax.broadcasted_iota(jnp.int32, (rows, page_rows), 1)
        ks, vs = _new_heads(kvnew_ref, n_heads, page_rows)
        s = _head_scores(q16_ref, ks, n_new) + ecol - _repeat_rows(et_ref[...], n_new)
        update(jnp.where(col_io <= jnp.bitwise_and(row_io, n_new - 1), s, -jnp.inf), [vs])

    later = _later_matrix(page_rows)
    run = run_ref[...]
    ecol = ecol_ref[...]
    s_chunks, v_chunks = [], []
    for r in range(pps):
        ks, vs = _page_heads(k_refs[r], v_refs[r], n_heads, page_rows)
        lf = _repeat_rows(lft_refs[r][...], n_new)
        s_chunks.append(_head_scores(q16_ref, ks, n_new) + ecol + _dot2_lhs(lf, later) + run)
        v_chunks.append(vs)
        run = run + jnp.sum(lf, axis=1, keepdims=True)
    run_ref[...] = run
    update(jnp.concatenate(s_chunks, axis=1), v_chunks)

    @pl.when(p == pl.num_programs(1) - 1)
    def _():
        _store_heads(acc_ref[...] / l_ref[...], og_ref, o_ref, n_heads, n_new)


def _sb_sample_kernel(*refs, n_heads, n_new, page_rows, pps, first):
    if first:
        q_ref, kvnew_ref = refs[1:3]
        rest = refs[3:]
    else:
        q_ref, acc_in_ref, run_in_ref = refs[2:5]
        rest = refs[5:]
    k_refs, v_refs = rest[:pps], rest[pps:2 * pps]
    if first:
        acc_out_ref, run_out_ref, q16_ref, acc_ref, run_ref = rest[2 * pps:]
    else:
        og_ref, o_ref, q16_ref, acc_ref, run_ref = rest[2 * pps:]
    p = pl.program_id(1)
    rows = n_heads * n_new
    q_rows = q16_ref.shape[1]
    later = _later_matrix(page_rows)

    def step(ks, vs, run, acc, mask):
        z = _head_scores(q16_ref, ks, n_new)
        lsm = _log_sigmoid(-z)
        if mask is not None:
            lsm = jnp.where(mask, lsm, 0.0)
        a = jnp.exp(lsm + z + _dot2_lhs(lsm, later) + run)
        if mask is not None:
            a = jnp.where(mask, a, 0.0)
        return run + jnp.sum(lsm, axis=1, keepdims=True), acc + _head_values(a, vs, n_new, q_rows)

    @pl.when(p == 0)
    def _():
        _place_queries(q_ref, q16_ref, n_heads, n_new)
        if first:
            row_io = lax.broadcasted_iota(jnp.int32, (rows, page_rows), 0)
            col_io = lax.broadcasted_iota(jnp.int32, (rows, page_rows), 1)
            ks, vs = _new_heads(kvnew_ref, n_heads, page_rows)
            run, acc = step(ks, vs, jnp.zeros((rows, 1), F32), jnp.zeros((rows, HEAD_DIM), F32),
                            col_io < jnp.bitwise_and(row_io, n_new - 1))
            run_ref[...] = run
            acc_ref[...] = acc
        else:
            run_ref[...] = run_in_ref[...]
            acc_ref[...] = acc_in_ref[...]

    @pl.when(jnp.max(run_ref[...]) > EXP_UNDERFLOW)
    def _():
        run, acc = run_ref[...], acc_ref[...]
        for r in range(pps):
            ks, vs = _page_heads(k_refs[r], v_refs[r], n_heads, page_rows)
            run, acc = step(ks, vs, run, acc, None)
        run_ref[...] = run
        acc_ref[...] = acc

    @pl.when(p == pl.num_programs(1) - 1)
    def _():
        if first:
            acc_out_ref[...] = acc_ref[...]
            run_out_ref[...] = run_ref[...]
        else:
            _store_heads(acc_ref[...], og_ref, o_ref, n_heads, n_new)


def _page_spec(shape, n_pages, pps, layer, r, col=0, step0=0, gated=False):
    def index(b, p, pt, *flags):
        slot = n_pages - 1 - ((p + step0) * pps + r)
        if gated:
            slot = jnp.where(flags[0][b] > 0, slot, r)
        return (pt[b * n_pages + slot], layer, 0, col)
    return pl.BlockSpec((None, None) + shape, index)


def _kv_page_specs(rows, n_pages, pps, layer, step0=0, gated=False):
    return [_page_spec((rows, HEAD_DIM), n_pages, pps, layer, r, col, step0, gated)
            for col in (0, 1) for r in range(pps)]


def _decode_common(page_table, q3, cache_kv, n_heads):
    B, L, _ = q3.shape
    n_pages = page_table.shape[1]
    page_rows = cache_kv.shape[2] // n_heads
    pps = _divisor_tile(n_pages, PAGES_PER_STEP, align=1)
    assert L & (L - 1) == 0 and L % SUBLANES == 0 and L <= page_rows and page_rows == LANES
    q_rows = -(-L // BF16_ROWS) * BF16_ROWS
    return B, L, n_pages, page_rows, pps, q_rows


def _fox_sample(page_table, q3, kvnew4, cum3, cum_t, cache_kv, cache_lft, out_gain, layer, n_heads):
    B, L, n_pages, page_rows, pps, q_rows = _decode_common(page_table, q3, cache_kv, n_heads)
    rows = n_heads * L
    qw = n_heads * HEAD_DIM
    lf_shape = (n_heads, page_rows)
    grid_spec = pltpu.PrefetchScalarGridSpec(
        num_scalar_prefetch=1,
        grid=(B, n_pages // pps),
        in_specs=[pl.BlockSpec((None, L, qw), lambda b, p, pt: (b, 0, 0)),
                  pl.BlockSpec((None, None, L, 2 * qw), lambda b, p, pt: (b, layer, 0, 0)),
                  pl.BlockSpec((None, L, LANES), lambda b, p, pt: (b, 0, 0)),
                  pl.BlockSpec((None, n_heads, LANES), lambda b, p, pt: (b, 0, 0))]
        + _kv_page_specs(page_rows * n_heads, n_pages, pps, layer)
        + [_page_spec(lf_shape, n_pages, pps, layer, r) for r in range(pps)]
        + [pl.BlockSpec((1, qw), lambda b, p, pt: (0, 0))],
        out_specs=pl.BlockSpec((None, L, qw), lambda b, p, pt: (b, 0, 0)),
        scratch_shapes=[pltpu.VMEM((n_heads, q_rows, HEAD_DIM), BF16), pltpu.VMEM((rows, 1), F32),
                        pltpu.VMEM((rows, 1), F32), pltpu.VMEM((rows, HEAD_DIM), F32),
                        pltpu.VMEM((rows, 1), F32), pltpu.VMEM((rows, 1), F32)])
    return pl.pallas_call(
        functools.partial(_fox_sample_kernel, n_heads=n_heads, n_new=L, page_rows=page_rows, pps=pps),
        grid_spec=grid_spec,
        out_shape=jax.ShapeDtypeStruct((B, L, qw), BF16),
        compiler_params=_cparams("parallel", "arbitrary"),
        name="fox_sample",
    )(page_table.reshape(-1), q3, kvnew4, cum3, cum_t, *([cache_kv] * (2 * pps)),
      *([cache_lft] * pps), out_gain)


def _sb_sample(page_table, q3, kvnew4, cache_kv, out_gain, layer, n_heads):
    B, L, n_pages, page_rows, pps, q_rows = _decode_common(page_table, q3, cache_kv, n_heads)
    rows = n_heads * L
    qw = n_heads * HEAD_DIM
    n_steps = n_pages // pps
    assert n_steps >= 2
    kernel_args = dict(n_heads=n_heads, n_new=L, page_rows=page_rows, pps=pps)
    scratch = [pltpu.VMEM((n_heads, q_rows, HEAD_DIM), BF16), pltpu.VMEM((rows, HEAD_DIM), F32),
               pltpu.VMEM((rows, 1), F32)]
    q_spec = pl.BlockSpec((None, L, qw), lambda b, p, *_: (b, 0, 1))
    acc_spec = pl.BlockSpec((None, rows, HEAD_DIM), lambda b, p, *_: (b, 0, 0))
    run_spec = pl.BlockSpec((None, rows, 1), lambda b, p, *_: (b, 0, 0))
    pt = page_table.reshape(-1)

    acc, run = pl.pallas_call(
        functools.partial(_sb_sample_kernel, first=True, **kernel_args),
        grid_spec=pltpu.PrefetchScalarGridSpec(
            num_scalar_prefetch=1, grid=(B, 1),
            in_specs=[q_spec, pl.BlockSpec((None, None, L, 2 * qw), lambda b, p, pt: (b, layer, 0, 0))]
            + _kv_page_specs(page_rows * n_heads, n_pages, pps, layer),
            out_specs=[acc_spec, run_spec], scratch_shapes=scratch),
        out_shape=[jax.ShapeDtypeStruct((B, rows, HEAD_DIM), F32),
                   jax.ShapeDtypeStruct((B, rows, 1), F32)],
        compiler_params=_cparams("parallel", "arbitrary"),
        name="sb_sample_head",
    )(pt, q3, kvnew4, *([cache_kv] * (2 * pps)))

    alive = (jnp.max(run, axis=(1, 2)) > EXP_UNDERFLOW).astype(jnp.int32)
    return pl.pallas_call(
        functools.partial(_sb_sample_kernel, first=False, **kernel_args),
        grid_spec=pltpu.PrefetchScalarGridSpec(
            num_scalar_prefetch=2, grid=(B, n_steps - 1),
            in_specs=[q_spec, acc_spec, run_spec]
            + _kv_page_specs(page_rows * n_heads, n_pages, pps, layer, step0=1, gated=True)
            + [pl.BlockSpec((1, qw), lambda b, p, *_: (0, 1))],
            out_specs=pl.BlockSpec((None, L, qw), lambda b, p, *_: (b, 0, 0)),
            scratch_shapes=scratch),
        out_shape=jax.ShapeDtypeStruct((B, L, qw), BF16),
        compiler_params=_cparams("parallel", "arbitrary"),
        name="sb_sample_tail",
    )(pt, alive, q3, acc, run, *([cache_kv] * (2 * pps)), out_gain)


def _out_proj_kernel(x_ref, a_ref, b_ref, c_ref, wa_ref, wb_ref, wc_ref, o_ref):
    o_ref[...] = (x_ref[...] + _dot(a_ref[...], wa_ref[...]) + _dot(b_ref[...], wb_ref[...])
                  + _dot(c_ref[...], wc_ref[...]))


def _out_proj(x, oa, ob, oc, w, layer):
    T, D = x.shape
    tm, tn = _divisor_tile(T, ROW_TILE), _divisor_tile(D, COL_TILE)
    wa, wb, wc = oa.shape[1], ob.shape[1], oc.shape[1]
    assert wa % wb == 0 and (wa + wb) % wc == 0

    def act(a):
        return pl.BlockSpec((tm, a.shape[1]), lambda i, j: (i, 0))

    def band(rows, row_blk):
        return pl.BlockSpec((None, rows, tn), lambda i, j: (layer, row_blk, j))

    return pl.pallas_call(
        _out_proj_kernel,
        grid=(T // tm, D // tn),
        in_specs=[pl.BlockSpec((tm, tn), lambda i, j: (i, j)), act(oa), act(ob), act(oc),
                  band(wa, 0), band(wb, wa // wb), band(wc, (wa + wb) // wc)],
        out_specs=pl.BlockSpec((tm, tn), lambda i, j: (i, j)),
        out_shape=jax.ShapeDtypeStruct((T, D), F32),
        compiler_params=_cparams("parallel", "arbitrary"),
        name="out_proj",
    )(x, oa, ob, oc, w, w, w)


def _ple_kernel(x_ref, h_ref, wg_ref, p_ref, wp_ref, o_ref):
    gate = jax.nn.sigmoid(_dot(h_ref[...], wg_ref[...]))
    emb = _dot(p_ref[...].astype(BF16), wp_ref[...])
    o_ref[...] = x_ref[...] + gate * emb


def _ple(x, h, wg, p, wp, layer):
    T, D = x.shape
    tm, tn = _divisor_tile(T, ROW_TILE), _divisor_tile(D, COL_TILE)
    return pl.pallas_call(
        _ple_kernel,
        grid=(T // tm, D // tn),
        in_specs=[pl.BlockSpec((tm, tn), lambda i, j: (i, j)),
                  pl.BlockSpec((tm, D), lambda i, j: (i, 0)),
                  pl.BlockSpec((None, D, tn), lambda i, j: (layer, 0, j)),
                  pl.BlockSpec((tm, p.shape[1]), lambda i, j: (i, 0)),
                  pl.BlockSpec((None, p.shape[1], tn), lambda i, j: (layer, 0, j))],
        out_specs=pl.BlockSpec((tm, tn), lambda i, j: (i, j)),
        out_shape=jax.ShapeDtypeStruct((T, D), F32),
        compiler_params=_cparams("parallel", "arbitrary"),
        name="ple",
    )(x, h, wg, p, wp)


def _aligned_w_in(w_in, hf, hs, hg):
    fw, sw, gw = hf * HEAD_DIM, hs * HEAD_DIM, hg * HEAD_DIM
    gate0 = 3 * fw
    w = jnp.concatenate([w_in[..., :gate0], w_in[..., gate0 + hf:],
                         jnp.pad(w_in[..., gate0:gate0 + hf], ((0, 0), (0, 0), (0, LANES - hf)))],
                        axis=-1).astype(BF16)
    cols, o = {}, 0
    for name, size in (("fq", fw), ("fk", fw), ("fv", fw), ("sq", sw), ("sk", sw), ("sv", sw),
                       ("hq", gw), ("hf", gw), ("hi", gw), ("hg", gw), ("ff", LANES)):
        cols[name] = o
        o += size
    return w, cols


def kernel(x_prompt, x_sample, p_prompt, p_sample, cache_fox_kv, cache_fox_logf, cache_sb_kv, state_hgrn,
           page_table, ffn1_norm, ffn1_w1, ffn1_w3, ffn1_w2, mix_norm, w_in, fox_f_bias, fox_q_norm,
           fox_k_norm, sb_q_norm, sb_k_norm, hg_lower_bound, out_norm, w_out, ffn2_norm, ffn2_w1, ffn2_w3,
           ffn2_w2, ple_norm, w_ple_gate, w_ple_proj):
    depth = w_in.shape[0]
    B, S, D = x_prompt.shape
    Bs, L, _ = x_sample.shape
    hf, hs = fox_f_bias.shape[1], cache_sb_kv.shape[3]
    hg = state_hgrn.shape[2]
    assert hf == hs
    fw, sw, gw = hf * HEAD_DIM, hs * HEAD_DIM, hg * HEAD_DIM
    n_pool, _, page_rows, _, _ = cache_fox_kv.shape

    w_in_b, cols = _aligned_w_in(w_in, hf, hs, hg)
    ffn1 = (ffn1_w1.astype(BF16), ffn1_w3.astype(BF16), ffn1_w2.astype(BF16))
    ffn2 = (ffn2_w1.astype(BF16), ffn2_w3.astype(BF16), ffn2_w2.astype(BF16))
    w_out_b, w_gate_b, w_emb_b = w_out.astype(BF16), w_ple_gate.astype(BF16), w_ple_proj.astype(BF16)

    gamma = jnp.cumsum(jax.nn.softmax(hg_lower_bound.astype(F32), axis=0), axis=0)
    cache_fkv = cache_fox_kv.reshape(n_pool, depth, page_rows * hf, 2 * HEAD_DIM)
    cache_skv = cache_sb_kv.reshape(n_pool, depth, page_rows * hs, 2 * HEAD_DIM)
    cache_lft = jnp.swapaxes(cache_fox_logf.astype(F32), 2, 3)

    xp = x_prompt.reshape(B * S, D)
    xs = x_sample.reshape(Bs * L, D)
    hg_pad = _divisor_tile(max(L, HG_CHUNK), HG_CHUNK)
    pf_kv = psb_kv = sf_kv = ssb_kv = None
    logf_p, logf_s, state_p, state_s = [], [], [], []
    scale = HEAD_DIM ** -0.5

    for l in range(depth):
        lb = (gamma[l - 1] if l > 0 else jnp.zeros_like(gamma[0]))[None, :]
        out_gain = out_norm[l][None, :]
        q_gain = jnp.concatenate([jnp.tile(fox_q_norm[l] * scale, hf),
                                  jnp.tile(sb_q_norm[l] * scale, hs)])[None, :]
        f_bias = jnp.pad(fox_f_bias[l], (0, LANES - hf))[None, :]

        def pre_mix(x, batch, seq, fkv_prev, skv_prev):
            x, h = _ffn(x, ffn1_norm[l][None, :], *ffn1, l, next_gain=mix_norm[l][None, :])
            fkv = _proj_kv(h, w_in_b, l, cols["fk"], cols["fv"], fw, fox_k_norm[l][None, :],
                           (batch, depth, seq, fkv_prev))
            skv = _proj_kv(h, w_in_b, l, cols["sk"], cols["sv"], sw, sb_k_norm[l][None, :],
                           (batch, depth, seq, skv_prev))
            q = _proj_q(h, w_in_b, l, [cols["fq"], cols["sq"]], fw, q_gain).reshape(batch, seq, fw + sw)
            bm = _proj_plain(h, w_in_b, l, [cols["hq"], cols["hi"], cols["hg"]], gw)
            glogf, gk = _proj_hgrn_gate(h, w_in_b, l, cols["hf"], gw, lb)
            flogf, fcum = _proj_fox_gate(h, w_in_b, l, cols["ff"], f_bias, seq)
            return (x, fkv, skv, q, bm.reshape(batch, seq, 3 * gw), glogf.reshape(batch, seq, gw),
                    gk.reshape(batch, seq, gw), flogf.reshape(batch, seq, LANES),
                    fcum.reshape(batch, seq, LANES))

        def post_mix(x, o_fox, o_sb, o_hg, p_l):
            x = _out_proj(x, o_fox, o_sb, o_hg, w_out_b, l)
            x, h = _ffn(x, ffn2_norm[l][None, :], *ffn2, l, next_gain=ple_norm[l][None, :])
            return _ple(x, h, w_gate_b, p_l, w_emb_b, l)

        xp, pf_kv, psb_kv, q, bm, glogf, gk, flogf, fcum = pre_mix(xp, B, S, pf_kv, psb_kv)
        cum_t = jnp.swapaxes(fcum[:, :, :hf], 1, 2)
        o_fox = _fox_prompt(q, pf_kv, l, fcum, cum_t, out_gain, hf)
        o_sb = _sb_prompt(q, psb_kv, l, out_gain, hs, hf, hf)
        o_hg, st = _hgrn(bm, glogf, gk, out_gain, hg, fw + sw, None)
        xp = post_mix(xp, o_fox.reshape(B * S, fw), o_sb.reshape(B * S, sw), o_hg.reshape(B * S, gw),
                      p_prompt[l].reshape(B * S, -1))
        logf_p.append(flogf[:, :, :hf])
        state_p.append(st)

        xs, sf_kv, ssb_kv, q, bm, glogf, gk, flogf, fcum = pre_mix(xs, Bs, L, sf_kv, ssb_kv)
        cum_t = jnp.pad(jnp.swapaxes(fcum[:, :, :hf], 1, 2), ((0, 0), (0, 0), (0, LANES - L)))
        o_fox = _fox_sample(page_table, q, sf_kv, fcum, cum_t, cache_fkv, cache_lft, out_gain, l, hf)
        o_sb = _sb_sample(page_table, q, ssb_kv, cache_skv, out_gain, l, hs)

        def pad_tokens(z):
            return jnp.pad(z, ((0, 0), (0, hg_pad - L), (0, 0)))

        o_hg, st = _hgrn(pad_tokens(bm), pad_tokens(glogf), pad_tokens(gk), out_gain, hg, fw + sw,
                         state_hgrn[l].astype(F32))
        xs = post_mix(xs, o_fox.reshape(Bs * L, fw), o_sb.reshape(Bs * L, sw),
                      o_hg[:, :L].reshape(Bs * L, gw), p_sample[l].reshape(Bs * L, -1))
        logf_s.append(flogf[:, :, :hf])
        state_s.append(st)

    return (xp.reshape(B, S, D), xs.reshape(Bs, L, D),
            pf_kv.reshape(B, depth, S, hf, 2 * HEAD_DIM), jnp.stack(logf_p, axis=1),
            psb_kv.reshape(B, depth, S, hs, 2 * HEAD_DIM), jnp.stack(state_p, axis=0),
            sf_kv.reshape(Bs, depth, L, hf, 2 * HEAD_DIM), jnp.stack(logf_s, axis=1),
            ssb_kv.reshape(Bs, depth, L, hs, 2 * HEAD_DIM), jnp.stack(state_s, axis=0))
```

```python
import functools

import jax
import jax.numpy as jnp
from jax import lax
from jax.experimental import pallas as pl
from jax.experimental.pallas import tpu as pltpu

F32 = jnp.float32
BF16 = jnp.bfloat16

HEAD_DIM = 128
EPS = 1e-6
LANES = 128
SUBLANES = 8
BF16_ROWS = 16
VMEM_LIMIT_BYTES = 56 << 20
VMEM_LIMIT_FFN_BYTES = 60 << 20
ROW_TILE = 512
COL_TILE = 1024
FF_TILE = 256
GATE_CHUNK = 256
ATTN_BLOCK = 256
ATTN_HEADS_PER_STEP = 8
HG_CHUNK = 128
HG_HEADS_PER_STEP = 4
PAGES_PER_STEP = 16
SB_PAGES_PER_STEP = 8
EXP_UNDERFLOW = -104.0


def _cparams(*sem, vmem=VMEM_LIMIT_BYTES):
    return pltpu.CompilerParams(dimension_semantics=sem, vmem_limit_bytes=vmem)


def _divisor_tile(n, pref, align=LANES):
    if n <= pref:
        return n
    for t in range(pref - pref % align, 0, -align):
        if n % t == 0:
            return t
    raise ValueError((n, pref, align))


def _dot(a, b):
    return jnp.dot(a, b, preferred_element_type=F32)


def _dot_nt(a, b):
    return lax.dot_general(a, b, (((1,), (1,)), ((), ())), preferred_element_type=F32)


def _dot_tn(a, b):
    return lax.dot_general(a, b, (((0,), (0,)), ((), ())), preferred_element_type=F32)


def _dot3_rhs(m_bf16, x):
    hi = x.astype(BF16)
    r1 = x - hi.astype(F32)
    mid = r1.astype(BF16)
    lo = (r1 - mid.astype(F32)).astype(BF16)
    return _dot(m_bf16, hi) + _dot(m_bf16, mid) + _dot(m_bf16, lo)


def _dot2_lhs(x, m_bf16):
    hi = x.astype(BF16)
    lo = (x - hi.astype(F32)).astype(BF16)
    return _dot(hi, m_bf16) + _dot(lo, m_bf16)


def _log_sigmoid(x):
    return jnp.minimum(x, 0.0) - jnp.log1p(jnp.exp(-jnp.abs(x)))


def _rms_rows(x, gain):
    return x * lax.rsqrt(jnp.mean(x * x, axis=-1, keepdims=True) + EPS) * gain


def _ones_where(cond):
    return jnp.where(cond, 1.0, 0.0).astype(BF16)


def _later_matrix(n):
    j_io = lax.broadcasted_iota(jnp.int32, (n, n), 0)
    s_io = lax.broadcasted_iota(jnp.int32, (n, n), 1)
    return _ones_where(j_io > s_io)


def _pick_lane(x, idx):
    lane = lax.broadcasted_iota(jnp.int32, x.shape, 1)
    return jnp.sum(jnp.where(lane == idx, x, 0.0), axis=1, keepdims=True)


def _pick_row(x, idx):
    row = lax.broadcasted_iota(jnp.int32, x.shape, 0)
    return jnp.sum(jnp.where(row == idx, x, 0.0), axis=0, keepdims=True)


def _repeat_rows(x, reps):
    return jnp.concatenate(
        [jnp.broadcast_to(x[h:h + 1, :], (reps, x.shape[1])) for h in range(x.shape[0])], axis=0)


def _ffn_kernel(*refs, emit_norm):
    if emit_norm:
        x_ref, g_ref, w1_ref, w3_ref, w2_ref, g2_ref, o_ref, h2_ref, h_ref = refs
    else:
        x_ref, g_ref, w1_ref, w3_ref, w2_ref, o_ref, h_ref = refs

    @pl.when(pl.program_id(1) == 0)
    def _():
        x = x_ref[...]
        h_ref[...] = _rms_rows(x, g_ref[...]).astype(BF16)
        o_ref[...] = x

    h = h_ref[...]
    a = _dot(h, w1_ref[...])
    b = _dot(h, w3_ref[...])
    gated = (a * jax.nn.sigmoid(a) * b * 0.5).astype(BF16)
    o_ref[...] += _dot(gated, w2_ref[...])

    if emit_norm:
        @pl.when(pl.program_id(1) == pl.num_programs(1) - 1)
        def _():
            h2_ref[...] = _rms_rows(o_ref[...], g2_ref[...]).astype(BF16)


def _ffn(x, gain, w1, w3, w2, layer, next_gain=None):
    T, D = x.shape
    F = w1.shape[2]
    tm = _divisor_tile(T, ROW_TILE)
    tf = _divisor_tile(F, FF_TILE)
    emit = next_gain is not None
    row = pl.BlockSpec((1, D), lambda i, j: (0, 0))
    tile = pl.BlockSpec((tm, D), lambda i, j: (i, 0))
    in_specs = [pl.BlockSpec((tm, D), lambda i, j: (i, 0), pipeline_mode=pl.Buffered(1)), row,
                pl.BlockSpec((None, D, tf), lambda i, j: (layer, 0, j)),
                pl.BlockSpec((None, D, tf), lambda i, j: (layer, 0, j)),
                pl.BlockSpec((None, tf, D), lambda i, j: (layer, j, 0))]
    args = [x, gain, w1, w3, w2]
    out_specs, out_shape = tile, jax.ShapeDtypeStruct((T, D), F32)
    if emit:
        in_specs.append(row)
        args.append(next_gain)
        out_specs = [tile, tile]
        out_shape = [out_shape, jax.ShapeDtypeStruct((T, D), BF16)]
    return pl.pallas_call(
        functools.partial(_ffn_kernel, emit_norm=emit),
        grid=(T // tm, F // tf),
        in_specs=in_specs, out_specs=out_specs, out_shape=out_shape,
        scratch_shapes=[pltpu.VMEM((tm, D), BF16)],
        compiler_params=_cparams("parallel", "arbitrary", vmem=VMEM_LIMIT_FFN_BYTES),
        name="ffn",
    )(*args)


def _proj_q_kernel(h_ref, w_ref, hgain_ref, o_ref):
    z = _dot(h_ref[...], w_ref[...])
    for c in range(z.shape[1] // HEAD_DIM):
        sl = slice(c * HEAD_DIM, (c + 1) * HEAD_DIM)
        o_ref[:, sl] = _rms_rows(z[:, sl], hgain_ref[:, sl])


def _proj_kv_kernel(*refs):
    h_ref, wk_ref, wv_ref, kgain_ref = refs[:4]
    o_ref = refs[-1]
    h = h_ref[...]
    zk = _dot(h, wk_ref[...])
    zv = _dot(h, wv_ref[...])
    lead = o_ref.shape[:-1]
    for c in range(zk.shape[1] // HEAD_DIM):
        sl = slice(c * HEAD_DIM, (c + 1) * HEAD_DIM)
        k_out = slice(2 * c * HEAD_DIM, (2 * c + 1) * HEAD_DIM)
        v_out = slice((2 * c + 1) * HEAD_DIM, (2 * c + 2) * HEAD_DIM)
        o_ref[..., k_out] = _rms_rows(zk[:, sl], kgain_ref[...]).reshape(*lead, HEAD_DIM)
        o_ref[..., v_out] = zv[:, sl].reshape(*lead, HEAD_DIM)


def _proj_plain_kernel(h_ref, w_ref, o_ref):
    o_ref[...] = _dot(h_ref[...], w_ref[...])


def _proj_hgrn_gate_kernel(h_ref, w_ref, lb_ref, logf_ref, k_ref):
    h = h_ref[...]
    for c in range(0, w_ref.shape[1], GATE_CHUNK):
        sl = slice(c, c + GATE_CHUNK)
        z = _dot(h, w_ref[:, sl])
        lb = lb_ref[:, sl]
        e = jnp.exp(-jnp.abs(z))
        r = 1.0 / (1.0 + e)
        a = jnp.log1p(-lb) + jnp.minimum(z, 0.0) - jnp.log1p(e)
        b = jnp.log(lb)
        logf_ref[:, sl] = jnp.maximum(a, b) + jnp.log1p(jnp.exp(-jnp.abs(a - b)))
        k_ref[:, sl] = (1.0 - lb) * jnp.where(z >= 0.0, e * r, r)


def _proj_fox_gate_kernel(h_ref, w_ref, bias_ref, logf_ref, cum_ref, carry_ref, *, seq, tm):
    i = pl.program_id(0)
    logf = _log_sigmoid(_dot(h_ref[...], w_ref[...]) + bias_ref[...])
    logf_ref[...] = logf
    t = lax.broadcasted_iota(jnp.int32, (tm, tm), 0)
    s = lax.broadcasted_iota(jnp.int32, (tm, tm), 1)
    if seq >= tm:
        @pl.when(i % (seq // tm) == 0)
        def _():
            carry_ref[...] = jnp.zeros_like(carry_ref)

        cum = _dot3_rhs(_ones_where(s <= t), logf) + carry_ref[...]
        carry_ref[...] = cum[tm - 1:tm, :]
    else:
        shift = seq.bit_length() - 1
        same = lax.shift_right_logical(t, shift) == lax.shift_right_logical(s, shift)
        cum = _dot3_rhs(_ones_where(same & (s <= t)), logf)
    cum_ref[...] = cum


def _proj_in_specs(tm, D, tn, layer, seg_cols, seg_width):
    per = seg_width // tn
    starts = [c // tn for c in seg_cols]
    assert seg_width % tn == 0 and all(c % tn == 0 for c in seg_cols)

    def w_index(i, j):
        blk = starts[-1]
        for k in range(len(starts) - 2, -1, -1):
            blk = jnp.where(j // per == k, starts[k], blk)
        return (layer, 0, blk + j % per)

    return [pl.BlockSpec((tm, D), lambda i, j: (i, 0)), pl.BlockSpec((None, D, tn), w_index)]


def _proj_tiles(h, seg_cols, seg_width):
    T, D = h.shape
    N = len(seg_cols) * seg_width
    return T, D, N, _divisor_tile(T, ROW_TILE), _divisor_tile(seg_width, COL_TILE)


def _proj_q(h, w, layer, seg_cols, seg_width, hgain):
    T, D, N, tm, tn = _proj_tiles(h, seg_cols, seg_width)
    return pl.pallas_call(
        _proj_q_kernel,
        grid=(T // tm, N // tn),
        in_specs=_proj_in_specs(tm, D, tn, layer, seg_cols, seg_width)
        + [pl.BlockSpec((1, tn), lambda i, j: (0, j))],
        out_specs=pl.BlockSpec((tm, tn), lambda i, j: (i, j)),
        out_shape=jax.ShapeDtypeStruct((T, N), F32),
        compiler_params=_cparams("parallel", "arbitrary"),
        name="proj_q",
    )(h, w, hgain)


def _proj_kv(h, w, layer, k_col, v_col, width, kgain, dest):
    T, D = h.shape
    tm, tk = _divisor_tile(T, ROW_TILE), _divisor_tile(width, COL_TILE // 2)
    assert k_col % tk == 0 and v_col % tk == 0
    batch, depth, rows, prev = dest
    assert batch * rows == T
    in_specs = [pl.BlockSpec((tm, D), lambda i, j: (i, 0)),
                pl.BlockSpec((None, D, tk), lambda i, j: (layer, 0, k_col // tk + j)),
                pl.BlockSpec((None, D, tk), lambda i, j: (layer, 0, v_col // tk + j)),
                pl.BlockSpec((1, HEAD_DIM), lambda i, j: (0, 0))]
    args = [h, w, w, kgain]
    if rows >= tm:
        per = rows // tm
        out_spec = pl.BlockSpec((1, None, tm, 2 * tk), lambda i, j: (i // per, layer, i % per, j))
    else:
        assert tm % rows == 0 and rows % SUBLANES == 0
        out_spec = pl.BlockSpec((tm // rows, None, rows, 2 * tk), lambda i, j: (i, layer, 0, j))
    aliases = {}
    if prev is not None:
        in_specs.append(pl.BlockSpec(memory_space=pl.ANY))
        args.append(prev)
        aliases = {len(args) - 1: 0}
    return pl.pallas_call(
        _proj_kv_kernel,
        grid=(T // tm, width // tk),
        in_specs=in_specs, out_specs=out_spec,
        out_shape=jax.ShapeDtypeStruct((batch, depth, rows, 2 * width), F32),
        input_output_aliases=aliases,
        compiler_params=_cparams("parallel", "arbitrary"),
        name="proj_kv",
    )(*args)


def _proj_plain(h, w, layer, seg_cols, seg_width):
    T, D, N, tm, tn = _proj_tiles(h, seg_cols, seg_width)
    return pl.pallas_call(
        _proj_plain_kernel,
        grid=(T // tm, N // tn),
        in_specs=_proj_in_specs(tm, D, tn, layer, seg_cols, seg_width),
        out_specs=pl.BlockSpec((tm, tn), lambda i, j: (i, j)),
        out_shape=jax.ShapeDtypeStruct((T, N), F32),
        compiler_params=_cparams("parallel", "arbitrary"),
        name="proj_plain",
    )(h, w)


def _proj_hgrn_gate(h, w, layer, col, width, lb):
    T, D, N, tm, tn = _proj_tiles(h, [col], width)
    out = pl.BlockSpec((tm, tn), lambda i, j: (i, j))
    return pl.pallas_call(
        _proj_hgrn_gate_kernel,
        grid=(T // tm, N // tn),
        in_specs=_proj_in_specs(tm, D, tn, layer, [col], width)
        + [pl.BlockSpec((1, tn), lambda i, j: (0, j))],
        out_specs=[out, out],
        out_shape=[jax.ShapeDtypeStruct((T, N), F32)] * 2,
        compiler_params=_cparams("parallel", "arbitrary"),
        name="proj_hgrn_gate",
    )(h, w, lb)


def _proj_fox_gate(h, w, layer, col, bias, seq):
    T, D = h.shape
    tm = _divisor_tile(T, ROW_TILE)
    assert (seq % tm == 0) or (tm % seq == 0 and seq & (seq - 1) == 0)
    assert col % LANES == 0
    out = pl.BlockSpec((tm, LANES), lambda i: (i, 0))
    return pl.pallas_call(
        functools.partial(_proj_fox_gate_kernel, seq=seq, tm=tm),
        grid=(T // tm,),
        in_specs=[pl.BlockSpec((tm, D), lambda i: (i, 0)),
                  pl.BlockSpec((None, D, LANES), lambda i: (layer, 0, col // LANES)),
                  pl.BlockSpec((1, LANES), lambda i: (0, 0))],
        out_specs=[out, out],
        out_shape=[jax.ShapeDtypeStruct((T, LANES), F32)] * 2,
        scratch_shapes=[pltpu.VMEM((1, LANES), F32)],
        compiler_params=_cparams("arbitrary"),
        name="proj_fox_gate",
    )(h, w, bias)


def _kv_block(kv_ref, off, blk, hh):
    k = kv_ref[pl.ds(off, blk), hh * 2 * HEAD_DIM:hh * 2 * HEAD_DIM + HEAD_DIM].astype(BF16)
    v = kv_ref[pl.ds(off, blk), hh * 2 * HEAD_DIM + HEAD_DIM:(hh + 1) * 2 * HEAD_DIM].astype(BF16)
    return k, v


def _fox_prompt_kernel(q_ref, kv_ref, ccol_ref, crow_ref, og_ref, o_ref, *, blk, hp):
    g = pl.program_id(1)
    i = pl.program_id(2)
    heads = range(hp)
    qs = [q_ref[:, hh * HEAD_DIM:(hh + 1) * HEAD_DIM].astype(BF16) for hh in heads]
    ccols = [_pick_lane(ccol_ref[...], g * hp + hh) for hh in heads]
    t_io = lax.broadcasted_iota(jnp.int32, (blk, blk), 0)
    s_io = lax.broadcasted_iota(jnp.int32, (blk, blk), 1)

    def block(j, carry, diagonal):
        off = pl.multiple_of(j * blk, blk)
        crows = crow_ref[:, pl.ds(off, blk)]
        out = []
        for hh in heads:
            m, l, acc = carry[hh]
            k, v = _kv_block(kv_ref, off, blk, hh)
            s = _dot_nt(qs[hh], k) + ccols[hh] - _pick_row(crows, g * hp + hh)
            if diagonal:
                s = jnp.where(s_io <= t_io, s, -jnp.inf)
            m_new = jnp.maximum(m, jnp.max(s, axis=1, keepdims=True))
            alpha = jnp.exp(m - m_new)
            p = jnp.exp(s - m_new)
            l = alpha * l + jnp.sum(p, axis=1, keepdims=True)
            acc = alpha * acc + _dot(p.astype(BF16), v)
            out.append((m_new, l, acc))
        return tuple(out)

    init = tuple((jnp.full((blk, 1), -jnp.inf, F32), jnp.zeros((blk, 1), F32),
                  jnp.zeros((blk, HEAD_DIM), F32)) for _ in heads)
    carry = lax.fori_loop(0, i, lambda j, c: block(j, c, False), init)
    carry = block(i, carry, True)
    for hh in heads:
        _, l, acc = carry[hh]
        sl = slice(hh * HEAD_DIM, (hh + 1) * HEAD_DIM)
        o_ref[:, sl] = _rms_rows(acc / l, og_ref[:, sl]).astype(BF16)


def _sb_prompt_kernel(q_ref, kv_ref, og_ref, o_ref, *, blk, hp):
    i = pl.program_id(2)
    heads = range(hp)
    qs = [q_ref[:, hh * HEAD_DIM:(hh + 1) * HEAD_DIM].astype(BF16) for hh in heads]
    later = _later_matrix(blk)
    t_io = lax.broadcasted_iota(jnp.int32, (blk, blk), 0)
    s_io = lax.broadcasted_iota(jnp.int32, (blk, blk), 1)
    strict = s_io < t_io

    def block(j, carry, diagonal):
        off = pl.multiple_of(j * blk, blk)
        out = []
        for hh in heads:
            run, acc = carry[hh]
            k, v = _kv_block(kv_ref, off, blk, hh)
            z = _dot_nt(qs[hh], k)
            lsm = _log_sigmoid(-z)
            if diagonal:
                lsm = jnp.where(strict, lsm, 0.0)
            a = jnp.exp(lsm + z + _dot2_lhs(lsm, later) + run)
            if diagonal:
                a = jnp.where(strict, a, 0.0)
            out.append((run + jnp.sum(lsm, axis=1, keepdims=True), acc + _dot(a.astype(BF16), v)))
        return tuple(out)

    def alive(carry):
        top = jnp.max(carry[0][0])
        for hh in range(1, hp):
            top = jnp.maximum(top, jnp.max(carry[hh][0]))
        return (top > EXP_UNDERFLOW).astype(jnp.int32)

    init = tuple((jnp.zeros((blk, 1), F32), jnp.zeros((blk, HEAD_DIM), F32)) for _ in heads)
    carry = block(i, init, True)

    def body(state):
        j, _, c = state
        c = block(j, c, False)
        return j - 1, alive(c), c

    _, _, carry = lax.while_loop(lambda st: jnp.logical_and(st[0] >= 0, st[1] > 0), body,
                                 (i - 1, alive(carry), carry))
    for hh in heads:
        sl = slice(hh * HEAD_DIM, (hh + 1) * HEAD_DIM)
        o_ref[:, sl] = _rms_rows(carry[hh][1], og_ref[:, sl]).astype(BF16)


def _attn_prompt_specs(S, blk, hp, layer, q_blk0, gain_blk0):
    qw, kvw = hp * HEAD_DIM, hp * 2 * HEAD_DIM
    q_spec = pl.BlockSpec((None, blk, qw), lambda b, g, i: (b, i, q_blk0 + g))
    kv_spec = pl.BlockSpec((None, None, S, kvw), lambda b, g, i: (b, layer, 0, g))
    gain_spec = pl.BlockSpec((1, qw), lambda b, g, i: (0, gain_blk0 + g))
    out_spec = pl.BlockSpec((None, blk, qw), lambda b, g, i: (b, i, g))
    return q_spec, kv_spec, gain_spec, out_spec


def _fox_prompt(q3, kv4, layer, cum3, cum_t, out_gain, n_heads):
    B, S, _ = q3.shape
    blk = _divisor_tile(S, ATTN_BLOCK)
    hp = _divisor_tile(n_heads, ATTN_HEADS_PER_STEP, align=1)
    q_spec, kv_spec, gain_spec, out_spec = _attn_prompt_specs(S, blk, hp, layer, 0, 0)
    return pl.pallas_call(
        functools.partial(_fox_prompt_kernel, blk=blk, hp=hp),
        grid=(B, n_heads // hp, S // blk),
        in_specs=[q_spec, kv_spec,
                  pl.BlockSpec((None, blk, LANES), lambda b, g, i: (b, i, 0)),
                  pl.BlockSpec((None, n_heads, S), lambda b, g, i: (b, 0, 0)),
                  gain_spec],
        out_specs=out_spec,
        out_shape=jax.ShapeDtypeStruct((B, S, n_heads * HEAD_DIM), BF16),
        compiler_params=_cparams("parallel", "parallel", "arbitrary"),
        name="fox_prompt",
    )(q3, kv4, cum3, cum_t, out_gain)


def _sb_prompt(q3, kv4, layer, out_gain, n_heads, q_head0, gain_head0):
    B, S, _ = q3.shape
    blk = _divisor_tile(S, ATTN_BLOCK)
    hp = _divisor_tile(n_heads, ATTN_HEADS_PER_STEP, align=1)
    assert q_head0 % hp == 0 and gain_head0 % hp == 0
    q_spec, kv_spec, gain_spec, out_spec = _attn_prompt_specs(S, blk, hp, layer, q_head0 // hp,
                                                              gain_head0 // hp)
    return pl.pallas_call(
        functools.partial(_sb_prompt_kernel, blk=blk, hp=hp),
        grid=(B, n_heads // hp, S // blk),
        in_specs=[q_spec, kv_spec, gain_spec],
        out_specs=out_spec,
        out_shape=jax.ShapeDtypeStruct((B, S, n_heads * HEAD_DIM), BF16),
        compiler_params=_cparams("parallel", "parallel", "arbitrary"),
        name="sb_prompt",
    )(q3, kv4, out_gain)


def _hgrn_kernel(*refs, chunk, heads, has_state):
    if has_state:
        q_ref, g_ref, k_ref, i_ref, gate_ref, og_ref, s0_ref, o_ref, sout_ref, st_ref = refs
    else:
        q_ref, g_ref, k_ref, i_ref, gate_ref, og_ref, o_ref, sout_ref, st_ref = refs
    c = pl.program_id(2)
    width = heads * HEAD_DIM
    groups = chunk // SUBLANES

    @pl.when(c == 0)
    def _():
        for hh in range(heads):
            if has_state:
                st_ref[hh] = s0_ref[hh].T
            else:
                st_ref[hh] = jnp.zeros((HEAD_DIM, HEAD_DIM), F32)

    q = q_ref[...]
    g = g_ref[...]
    kk = k_ref[...]
    iv = i_ref[...].astype(BF16)

    rig = lax.broadcasted_iota(jnp.int32, (SUBLANES, width), 0)
    zero_row = jnp.zeros((1, width), F32)
    gg, bg, carry = [], [], zero_row
    for gi in range(groups):
        rows = g[gi * SUBLANES:(gi + 1) * SUBLANES, :]
        local = jnp.zeros((SUBLANES, width), F32)
        for j in range(SUBLANES):
            local = local + jnp.where(rig >= j, rows[j:j + 1, :], 0.0)
        b_rows = local + carry
        carry = b_rows[SUBLANES - 1:SUBLANES, :]
        gg.append(rows)
        bg.append(b_rows)
    b_last = carry

    def b_at(gi, j):
        return zero_row if gi < 0 else bg[gi][j:j + 1, :]

    def pick(values):
        m = SUBLANES // len(values)
        out = values[-1]
        for k in range(len(values) - 2, -1, -1):
            out = jnp.where(rig < (k + 1) * m, values[k], out)
        return out

    levels = []
    m = chunk // 2
    while m >= 1:
        levels.append(m)
        m //= 2

    q_lvl, k_lvl = {}, {}
    for m in levels:
        q_rows, k_rows = [], []
        for gi in range(groups):
            qg = q[gi * SUBLANES:(gi + 1) * SUBLANES, :]
            kg = kk[gi * SUBLANES:(gi + 1) * SUBLANES, :]
            if m == 1:
                u, v = gg[gi], None
            elif m >= SUBLANES:
                r = m // SUBLANES
                first = r * (gi // r)
                if (gi // r) % 2 == 1:
                    q_rows.append(qg * jnp.exp(bg[gi] - b_at(first - 1, SUBLANES - 1)))
                    k_rows.append(jnp.zeros_like(kg))
                else:
                    q_rows.append(jnp.zeros_like(qg))
                    k_rows.append(kg * jnp.exp(b_at(first + r - 1, SUBLANES - 1) - bg[gi]))
                continue
            else:
                n = SUBLANES // m
                prev = pick([b_at(gi - 1, SUBLANES - 1)] + [b_at(gi, k * m - 1) for k in range(1, n)])
                last = pick([b_at(gi, k * m + m - 1) for k in range(n)])
                u = bg[gi] - prev
                v = last - bg[gi]
            q_rows.append(qg * jnp.exp(u))
            k_rows.append(kg if v is None else kg * jnp.exp(v))
        q_lvl[m] = jnp.concatenate(q_rows, axis=0).astype(BF16)
        k_lvl[m] = jnp.concatenate(k_rows, axis=0).astype(BF16)

    b_all = jnp.concatenate(bg, axis=0)
    q_state = (q * jnp.exp(b_all)).astype(BF16)
    k_state = (kk * jnp.exp(b_last - b_all)).astype(BF16)
    decay = jnp.exp(b_last)
    q_bf = q.astype(BF16)
    k_bf = kk.astype(BF16)

    t_io = lax.broadcasted_iota(jnp.int32, (chunk, chunk), 0)
    s_io = lax.broadcasted_iota(jnp.int32, (chunk, chunk), 1)
    masks = {}
    for m in levels:
        sh = (2 * m).bit_length() - 1
        same = lax.shift_right_logical(t_io, sh) == lax.shift_right_logical(s_io, sh)
        upper = jnp.bitwise_and(t_io, 2 * m - 1) >= m
        lower = jnp.bitwise_and(s_io, 2 * m - 1) < m
        masks[m] = same & upper & lower

    for hh in range(heads):
        sl = slice(hh * HEAD_DIM, (hh + 1) * HEAD_DIM)
        w = jnp.where(t_io == s_io, _dot_nt(q_bf[:, sl], k_bf[:, sl]), 0.0)
        for m in levels:
            w = w + jnp.where(masks[m], _dot_nt(q_lvl[m][:, sl], k_lvl[m][:, sl]), 0.0)
        st = st_ref[hh]
        o = _dot(w.astype(BF16), iv[:, sl]) + _dot_nt(q_state[:, sl], st.astype(BF16))
        st_new = st * decay[:, sl] + _dot_tn(iv[:, sl], k_state[:, sl])
        st_ref[hh] = st_new
        gate = gate_ref[:, sl]
        o_ref[:, sl] = (_rms_rows(o, og_ref[:, sl]) * (gate * jax.nn.sigmoid(gate))).astype(BF16)

    @pl.when(c == pl.num_programs(2) - 1)
    def _():
        for hh in range(heads):
            sout_ref[hh] = st_ref[hh].T


def _hgrn(bm3, logf3, k3, out_gain, n_heads, gain_col0, s0):
    B, S, W = logf3.shape
    chunk = _divisor_tile(S, HG_CHUNK)
    hb = _divisor_tile(n_heads, HG_HEADS_PER_STEP, align=1)
    wb = hb * HEAD_DIM
    ng = n_heads // hb
    assert gain_col0 % wb == 0

    def col(k):
        return pl.BlockSpec((None, chunk, wb), lambda b, h, c: (b, c, k * ng + h))

    in_specs = [col(0),
                pl.BlockSpec((None, chunk, wb), lambda b, h, c: (b, c, h)),
                pl.BlockSpec((None, chunk, wb), lambda b, h, c: (b, c, h)),
                col(1), col(2),
                pl.BlockSpec((1, wb), lambda b, h, c: (0, gain_col0 // wb + h))]
    args = [bm3, logf3, k3, bm3, bm3, out_gain]
    state_spec = pl.BlockSpec((None, hb, HEAD_DIM, HEAD_DIM), lambda b, h, c: (b, h, 0, 0))
    if s0 is not None:
        in_specs.append(state_spec)
        args.append(s0)
    return pl.pallas_call(
        functools.partial(_hgrn_kernel, chunk=chunk, heads=hb, has_state=s0 is not None),
        grid=(B, ng, S // chunk),
        in_specs=in_specs,
        out_specs=[pl.BlockSpec((None, chunk, wb), lambda b, h, c: (b, c, h)), state_spec],
        out_shape=[jax.ShapeDtypeStruct((B, S, W), BF16),
                   jax.ShapeDtypeStruct((B, n_heads, HEAD_DIM, HEAD_DIM), F32)],
        scratch_shapes=[pltpu.VMEM((hb, HEAD_DIM, HEAD_DIM), F32)],
        compiler_params=_cparams("parallel", "parallel", "arbitrary"),
        name="hgrn",
    )(*args)


def _place_queries(q_ref, q16_ref, n_heads, n_new):
    q16_ref[...] = jnp.zeros_like(q16_ref)
    for h in range(n_heads):
        q16_ref[h, 0:n_new, :] = q_ref[:, h * HEAD_DIM:(h + 1) * HEAD_DIM].astype(BF16)


def _page_heads(k_ref, v_ref, n_heads, page_rows):
    ks = [k_ref[pl.ds(h, page_rows, stride=n_heads), :].astype(BF16) for h in range(n_heads)]
    vs = [v_ref[pl.ds(h, page_rows, stride=n_heads), :].astype(BF16) for h in range(n_heads)]
    return ks, vs


def _new_heads(kvnew_ref, n_heads, page_rows):
    new = kvnew_ref[...]
    pad = jnp.zeros((page_rows - new.shape[0], new.shape[1]), F32)
    kv = jnp.concatenate([new, pad], axis=0).astype(BF16)
    ks = [kv[:, h * 2 * HEAD_DIM:h * 2 * HEAD_DIM + HEAD_DIM] for h in range(n_heads)]
    vs = [kv[:, h * 2 * HEAD_DIM + HEAD_DIM:(h + 1) * 2 * HEAD_DIM] for h in range(n_heads)]
    return ks, vs


def _head_scores(q16_ref, ks, n_new):
    return jnp.concatenate([_dot_nt(q16_ref[h], k)[:n_new] for h, k in enumerate(ks)], axis=0)


def _head_values(p, vs, n_new, q_rows):
    outs = []
    for h, v in enumerate(vs):
        ph = p[h * n_new:(h + 1) * n_new, :]
        if q_rows > n_new:
            ph = jnp.concatenate([ph, jnp.zeros((q_rows - n_new, ph.shape[1]), F32)], axis=0)
        outs.append(_dot(ph.astype(BF16), v)[:n_new])
    return jnp.concatenate(outs, axis=0)


def _store_heads(o, og_ref, o_ref, n_heads, n_new):
    for h in range(n_heads):
        sl = slice(h * HEAD_DIM, (h + 1) * HEAD_DIM)
        o_ref[:, sl] = _rms_rows(o[h * n_new:(h + 1) * n_new, :], og_ref[:, sl]).astype(BF16)


def _fox_sample_kernel(pt_ref, q_ref, kvnew_ref, e_ref, et_ref, *rest, n_heads, n_new, page_rows, pps):
    del pt_ref
    k_refs, v_refs, lft_refs = rest[:pps], rest[pps:2 * pps], rest[2 * pps:3 * pps]
    og_ref, o_ref, q16_ref, m_ref, l_ref, acc_ref, run_ref, ecol_ref = rest[3 * pps:]
    p = pl.program_id(1)
    rows = n_heads * n_new
    q_rows = q16_ref.shape[1]

    def update(s, v_chunks):
        m_old = m_ref[...]
        m_new = jnp.maximum(m_old, jnp.max(s, axis=1, keepdims=True))
        alpha = jnp.exp(m_old - m_new)
        pm = jnp.exp(s - m_new)
        l_ref[...] = alpha * l_ref[...] + jnp.sum(pm, axis=1, keepdims=True)
        acc = alpha * acc_ref[...]
        for c, vs in enumerate(v_chunks):
            acc = acc + _head_values(pm[:, c * page_rows:(c + 1) * page_rows], vs, n_new, q_rows)
        acc_ref[...] = acc
        m_ref[...] = m_new

    @pl.when(p == 0)
    def _():
        _place_queries(q_ref, q16_ref, n_heads, n_new)
        m_ref[...] = jnp.full_like(m_ref, -jnp.inf)
        l_ref[...] = jnp.zeros_like(l_ref)
        acc_ref[...] = jnp.zeros_like(acc_ref)
        run_ref[...] = jnp.zeros_like(run_ref)
        e = e_ref[...]
        ecol = jnp.concatenate([_pick_lane(e, h) for h in range(n_heads)], axis=0)
        ecol_ref[...] = ecol
        row_io = lax.broadcasted_iota(jnp.int32, (rows, page_rows), 0)
        col_io = lax.broadcasted_iota(jnp.int32, (rows, page_rows), 1)
        ks, vs = _new_heads(kvnew_ref, n_heads, page_rows)
        s = _head_scores(q16_ref, ks, n_new) + ecol - _repeat_rows(et_ref[...], n_new)
        update(jnp.where(col_io <= jnp.bitwise_and(row_io, n_new - 1), s, -jnp.inf), [vs])

    later = _later_matrix(page_rows)
    run = run_ref[...]
    ecol = ecol_ref[...]
    s_chunks, v_chunks = [], []
    for r in range(pps):
        ks, vs = _page_heads(k_refs[r], v_refs[r], n_heads, page_rows)
        lf = _repeat_rows(lft_refs[r][...], n_new)
        s_chunks.append(_head_scores(q16_ref, ks, n_new) + ecol + _dot2_lhs(lf, later) + run)
        v_chunks.append(vs)
        run = run + jnp.sum(lf, axis=1, keepdims=True)
    run_ref[...] = run
    update(jnp.concatenate(s_chunks, axis=1), v_chunks)

    @pl.when(p == pl.num_programs(1) - 1)
    def _():
        _store_heads(acc_ref[...] / l_ref[...], og_ref, o_ref, n_heads, n_new)


def _sb_sample_kernel(*refs, n_heads, n_new, page_rows, pps, first):
    if first:
        q_ref, kvnew_ref = refs[1:3]
        rest = refs[3:]
    else:
        q_ref, acc_in_ref, run_in_ref = refs[2:5]
        rest = refs[5:]
    k_refs, v_refs = rest[:pps], rest[pps:2 * pps]
    if first:
        acc_out_ref, run_out_ref, q16_ref, acc_ref, run_ref = rest[2 * pps:]
    else:
        og_ref, o_ref, q16_ref, acc_ref, run_ref = rest[2 * pps:]
    p = pl.program_id(1)
    rows = n_heads * n_new
    q_rows = q16_ref.shape[1]
    later = _later_matrix(page_rows)

    def step(ks, vs, run, acc, mask):
        z = _head_scores(q16_ref, ks, n_new)
        lsm = _log_sigmoid(-z)
        if mask is not None:
            lsm = jnp.where(mask, lsm, 0.0)
        a = jnp.exp(lsm + z + _dot2_lhs(lsm, later) + run)
        if mask is not None:
            a = jnp.where(mask, a, 0.0)
        return run + jnp.sum(lsm, axis=1, keepdims=True), acc + _head_values(a, vs, n_new, q_rows)

    @pl.when(p == 0)
    def _():
        _place_queries(q_ref, q16_ref, n_heads, n_new)
        if first:
            row_io = lax.broadcasted_iota(jnp.int32, (rows, page_rows), 0)
            col_io = lax.broadcasted_iota(jnp.int32, (rows, page_rows), 1)
            ks, vs = _new_heads(kvnew_ref, n_heads, page_rows)
            run, acc = step(ks, vs, jnp.zeros((rows, 1), F32), jnp.zeros((rows, HEAD_DIM), F32),
                            col_io < jnp.bitwise_and(row_io, n_new - 1))
            run_ref[...] = run
            acc_ref[...] = acc
        else:
            run_ref[...] = run_in_ref[...]
            acc_ref[...] = acc_in_ref[...]

    @pl.when(jnp.max(run_ref[...]) > EXP_UNDERFLOW)
    def _():
        run, acc = run_ref[...], acc_ref[...]
        for r in range(pps):
            ks, vs = _page_heads(k_refs[r], v_refs[r], n_heads, page_rows)
            run, acc = step(ks, vs, run, acc, None)
        run_ref[...] = run
        acc_ref[...] = acc

    @pl.when(p == pl.num_programs(1) - 1)
    def _():
        if first:
            acc_out_ref[...] = acc_ref[...]
            run_out_ref[...] = run_ref[...]
        else:
            _store_heads(acc_ref[...], og_ref, o_ref, n_heads, n_new)


def _sb_finalize_kernel(acc_ref, og_ref, o_ref, *, n_heads, n_new):
    _store_heads(acc_ref[...], og_ref, o_ref, n_heads, n_new)


def _page_spec(shape, n_pages, pps, layer, r, col=0, step0=0, gated=False):
    def index(b, p, pt, *flags):
        slot = n_pages - 1 - ((p + step0) * pps + r)
        if gated:
            slot = jnp.where(flags[0][b] > 0, slot, r)
        return (pt[b * n_pages + slot], layer, 0, col)
    return pl.BlockSpec((None, None) + shape, index)


def _kv_page_specs(rows, n_pages, pps, layer, step0=0, gated=False):
    return [_page_spec((rows, HEAD_DIM), n_pages, pps, layer, r, col, step0, gated)
            for col in (0, 1) for r in range(pps)]


def _decode_common(page_table, q3, cache_kv, n_heads, pages_per_step=PAGES_PER_STEP):
    B, L, _ = q3.shape
    n_pages = page_table.shape[1]
    page_rows = cache_kv.shape[2] // n_heads
    pps = _divisor_tile(n_pages, pages_per_step, align=1)
    assert L & (L - 1) == 0 and L % SUBLANES == 0 and L <= page_rows and page_rows == LANES
    q_rows = -(-L // BF16_ROWS) * BF16_ROWS
    return B, L, n_pages, page_rows, pps, q_rows


def _fox_sample(page_table, q3, kvnew4, cum3, cum_t, cache_kv, cache_lft, out_gain, layer, n_heads):
    B, L, n_pages, page_rows, pps, q_rows = _decode_common(page_table, q3, cache_kv, n_heads)
    rows = n_heads * L
    qw = n_heads * HEAD_DIM
    lf_shape = (n_heads, page_rows)
    grid_spec = pltpu.PrefetchScalarGridSpec(
        num_scalar_prefetch=1,
        grid=(B, n_pages // pps),
        in_specs=[pl.BlockSpec((None, L, qw), lambda b, p, pt: (b, 0, 0)),
                  pl.BlockSpec((None, None, L, 2 * qw), lambda b, p, pt: (b, layer, 0, 0)),
                  pl.BlockSpec((None, L, LANES), lambda b, p, pt: (b, 0, 0)),
                  pl.BlockSpec((None, n_heads, LANES), lambda b, p, pt: (b, 0, 0))]
        + _kv_page_specs(page_rows * n_heads, n_pages, pps, layer)
        + [_page_spec(lf_shape, n_pages, pps, layer, r) for r in range(pps)]
        + [pl.BlockSpec((1, qw), lambda b, p, pt: (0, 0))],
        out_specs=pl.BlockSpec((None, L, qw), lambda b, p, pt: (b, 0, 0)),
        scratch_shapes=[pltpu.VMEM((n_heads, q_rows, HEAD_DIM), BF16), pltpu.VMEM((rows, 1), F32),
                        pltpu.VMEM((rows, 1), F32), pltpu.VMEM((rows, HEAD_DIM), F32),
                        pltpu.VMEM((rows, 1), F32), pltpu.VMEM((rows, 1), F32)])
    return pl.pallas_call(
        functools.partial(_fox_sample_kernel, n_heads=n_heads, n_new=L, page_rows=page_rows, pps=pps),
        grid_spec=grid_spec,
        out_shape=jax.ShapeDtypeStruct((B, L, qw), BF16),
        compiler_params=_cparams("parallel", "arbitrary"),
        name="fox_sample",
    )(page_table.reshape(-1), q3, kvnew4, cum3, cum_t, *([cache_kv] * (2 * pps)),
      *([cache_lft] * pps), out_gain)


def _sb_sample(page_table, q3, kvnew4, cache_kv, out_gain, layer, n_heads):
    B, L, n_pages, page_rows, pps, q_rows = _decode_common(page_table, q3, cache_kv, n_heads,
                                                           SB_PAGES_PER_STEP)
    rows = n_heads * L
    qw = n_heads * HEAD_DIM
    n_steps = n_pages // pps
    assert n_steps >= 2
    kernel_args = dict(n_heads=n_heads, n_new=L, page_rows=page_rows, pps=pps)
    scratch = [pltpu.VMEM((n_heads, q_rows, HEAD_DIM), BF16), pltpu.VMEM((rows, HEAD_DIM), F32),
               pltpu.VMEM((rows, 1), F32)]
    q_spec = pl.BlockSpec((None, L, qw), lambda b, p, *_: (b, 0, 1))
    acc_spec = pl.BlockSpec((None, rows, HEAD_DIM), lambda b, p, *_: (b, 0, 0))
    run_spec = pl.BlockSpec((None, rows, 1), lambda b, p, *_: (b, 0, 0))
    pt = page_table.reshape(-1)

    acc, run = pl.pallas_call(
        functools.partial(_sb_sample_kernel, first=True, **kernel_args),
        grid_spec=pltpu.PrefetchScalarGridSpec(
            num_scalar_prefetch=1, grid=(B, 1),
            in_specs=[q_spec, pl.BlockSpec((None, None, L, 2 * qw), lambda b, p, pt: (b, layer, 0, 0))]
            + _kv_page_specs(page_rows * n_heads, n_pages, pps, layer),
            out_specs=[acc_spec, run_spec], scratch_shapes=scratch),
        out_shape=[jax.ShapeDtypeStruct((B, rows, HEAD_DIM), F32),
                   jax.ShapeDtypeStruct((B, rows, 1), F32)],
        compiler_params=_cparams("parallel", "arbitrary"),
        name="sb_sample_head",
    )(pt, q3, kvnew4, *([cache_kv] * (2 * pps)))

    alive = (jnp.max(run, axis=(1, 2)) > EXP_UNDERFLOW).astype(jnp.int32)
    out_shape = jax.ShapeDtypeStruct((B, L, qw), BF16)

    def remaining_pages():
        return pl.pallas_call(
            functools.partial(_sb_sample_kernel, first=False, **kernel_args),
            grid_spec=pltpu.PrefetchScalarGridSpec(
                num_scalar_prefetch=2, grid=(B, n_steps - 1),
                in_specs=[q_spec, acc_spec, run_spec]
                + _kv_page_specs(page_rows * n_heads, n_pages, pps, layer, step0=1, gated=True)
                + [pl.BlockSpec((1, qw), lambda b, p, *_: (0, 1))],
                out_specs=pl.BlockSpec((None, L, qw), lambda b, p, *_: (b, 0, 0)),
                scratch_shapes=scratch),
            out_shape=out_shape,
            compiler_params=_cparams("parallel", "arbitrary"),
            name="sb_sample_tail",
        )(pt, alive, q3, acc, run, *([cache_kv] * (2 * pps)), out_gain)

    def normalise_only():
        return pl.pallas_call(
            functools.partial(_sb_finalize_kernel, n_heads=n_heads, n_new=L),
            grid=(B,),
            in_specs=[pl.BlockSpec((None, rows, HEAD_DIM), lambda b: (b, 0, 0)),
                      pl.BlockSpec((1, qw), lambda b: (0, 1))],
            out_specs=pl.BlockSpec((None, L, qw), lambda b: (b, 0, 0)),
            out_shape=out_shape,
            compiler_params=_cparams("parallel"),
            name="sb_sample_finalize",
        )(acc, out_gain)

    return lax.cond(jnp.any(alive > 0), remaining_pages, normalise_only)


def _out_proj_kernel(x_ref, a_ref, b_ref, c_ref, wa_ref, wb_ref, wc_ref, o_ref):
    o_ref[...] = (x_ref[...] + _dot(a_ref[...], wa_ref[...]) + _dot(b_ref[...], wb_ref[...])
                  + _dot(c_ref[...], wc_ref[...]))


def _out_proj(x, oa, ob, oc, w, layer):
    T, D = x.shape
    tm, tn = _divisor_tile(T, ROW_TILE), _divisor_tile(D, COL_TILE)
    wa, wb, wc = oa.shape[1], ob.shape[1], oc.shape[1]
    assert wa % wb == 0 and (wa + wb) % wc == 0

    def act(a):
        return pl.BlockSpec((tm, a.shape[1]), lambda i, j: (i, 0))

    def band(rows, row_blk):
        return pl.BlockSpec((None, rows, tn), lambda i, j: (layer, row_blk, j))

    return pl.pallas_call(
        _out_proj_kernel,
        grid=(T // tm, D // tn),
        in_specs=[pl.BlockSpec((tm, tn), lambda i, j: (i, j)), act(oa), act(ob), act(oc),
                  band(wa, 0), band(wb, wa // wb), band(wc, (wa + wb) // wc)],
        out_specs=pl.BlockSpec((tm, tn), lambda i, j: (i, j)),
        out_shape=jax.ShapeDtypeStruct((T, D), F32),
        compiler_params=_cparams("parallel", "arbitrary"),
        name="out_proj",
    )(x, oa, ob, oc, w, w, w)


def _ple_kernel(x_ref, h_ref, wg_ref, p_ref, wp_ref, o_ref):
    gate = jax.nn.sigmoid(_dot(h_ref[...], wg_ref[...]))
    emb = _dot(p_ref[...].astype(BF16), wp_ref[...])
    o_ref[...] = x_ref[...] + gate * emb


def _ple(x, h, wg, p, wp, layer):
    T, D = x.shape
    tm, tn = _divisor_tile(T, ROW_TILE), _divisor_tile(D, COL_TILE)
    return pl.pallas_call(
        _ple_kernel,
        grid=(T // tm, D // tn),
        in_specs=[pl.BlockSpec((tm, tn), lambda i, j: (i, j)),
                  pl.BlockSpec((tm, D), lambda i, j: (i, 0)),
                  pl.BlockSpec((None, D, tn), lambda i, j: (layer, 0, j)),
                  pl.BlockSpec((tm, p.shape[1]), lambda i, j: (i, 0)),
                  pl.BlockSpec((None, p.shape[1], tn), lambda i, j: (layer, 0, j))],
        out_specs=pl.BlockSpec((tm, tn), lambda i, j: (i, j)),
        out_shape=jax.ShapeDtypeStruct((T, D), F32),
        compiler_params=_cparams("parallel", "arbitrary"),
        name="ple",
    )(x, h, wg, p, wp)


def _aligned_w_in(w_in, hf, hs, hg):
    fw, sw, gw = hf * HEAD_DIM, hs * HEAD_DIM, hg * HEAD_DIM
    gate0 = 3 * fw
    w = jnp.concatenate([w_in[..., :gate0], w_in[..., gate0 + hf:],
                         jnp.pad(w_in[..., gate0:gate0 + hf], ((0, 0), (0, 0), (0, LANES - hf)))],
                        axis=-1).astype(BF16)
    cols, o = {}, 0
    for name, size in (("fq", fw), ("fk", fw), ("fv", fw), ("sq", sw), ("sk", sw), ("sv", sw),
                       ("hq", gw), ("hf", gw), ("hi", gw), ("hg", gw), ("ff", LANES)):
        cols[name] = o
        o += size
    return w, cols


def kernel(x_prompt, x_sample, p_prompt, p_sample, cache_fox_kv, cache_fox_logf, cache_sb_kv, state_hgrn,
           page_table, ffn1_norm, ffn1_w1, ffn1_w3, ffn1_w2, mix_norm, w_in, fox_f_bias, fox_q_norm,
           fox_k_norm, sb_q_norm, sb_k_norm, hg_lower_bound, out_norm, w_out, ffn2_norm, ffn2_w1, ffn2_w3,
           ffn2_w2, ple_norm, w_ple_gate, w_ple_proj):
    depth = w_in.shape[0]
    B, S, D = x_prompt.shape
    Bs, L, _ = x_sample.shape
    hf, hs = fox_f_bias.shape[1], cache_sb_kv.shape[3]
    hg = state_hgrn.shape[2]
    assert hf == hs
    fw, sw, gw = hf * HEAD_DIM, hs * HEAD_DIM, hg * HEAD_DIM
    n_pool, _, page_rows, _, _ = cache_fox_kv.shape

    w_in_b, cols = _aligned_w_in(w_in, hf, hs, hg)
    ffn1 = (ffn1_w1.astype(BF16), ffn1_w3.astype(BF16), ffn1_w2.astype(BF16))
    ffn2 = (ffn2_w1.astype(BF16), ffn2_w3.astype(BF16), ffn2_w2.astype(BF16))
    w_out_b, w_gate_b, w_emb_b = w_out.astype(BF16), w_ple_gate.astype(BF16), w_ple_proj.astype(BF16)

    gamma = jnp.cumsum(jax.nn.softmax(hg_lower_bound.astype(F32), axis=0), axis=0)
    cache_fkv = cache_fox_kv.reshape(n_pool, depth, page_rows * hf, 2 * HEAD_DIM)
    cache_skv = cache_sb_kv.reshape(n_pool, depth, page_rows * hs, 2 * HEAD_DIM)
    cache_lft = jnp.swapaxes(cache_fox_logf.astype(F32), 2, 3)

    xp = x_prompt.reshape(B * S, D)
    xs = x_sample.reshape(Bs * L, D)
    hg_pad = _divisor_tile(max(L, HG_CHUNK), HG_CHUNK)
    pf_kv = psb_kv = sf_kv = ssb_kv = None
    logf_p, logf_s, state_p, state_s = [], [], [], []
    scale = HEAD_DIM ** -0.5

    for l in range(depth):
        lb = (gamma[l - 1] if l > 0 else jnp.zeros_like(gamma[0]))[None, :]
        out_gain = out_norm[l][None, :]
        q_gain = jnp.concatenate([jnp.tile(fox_q_norm[l] * scale, hf),
                                  jnp.tile(sb_q_norm[l] * scale, hs)])[None, :]
        f_bias = jnp.pad(fox_f_bias[l], (0, LANES - hf))[None, :]

        def pre_mix(x, batch, seq, fkv_prev, skv_prev):
            x, h = _ffn(x, ffn1_norm[l][None, :], *ffn1, l, next_gain=mix_norm[l][None, :])
            fkv = _proj_kv(h, w_in_b, l, cols["fk"], cols["fv"], fw, fox_k_norm[l][None, :],
                           (batch, depth, seq, fkv_prev))
            skv = _proj_kv(h, w_in_b, l, cols["sk"], cols["sv"], sw, sb_k_norm[l][None, :],
                           (batch, depth, seq, skv_prev))
            q = _proj_q(h, w_in_b, l, [cols["fq"], cols["sq"]], fw, q_gain).reshape(batch, seq, fw + sw)
            bm = _proj_plain(h, w_in_b, l, [cols["hq"], cols["hi"], cols["hg"]], gw)
            glogf, gk = _proj_hgrn_gate(h, w_in_b, l, cols["hf"], gw, lb)
            flogf, fcum = _proj_fox_gate(h, w_in_b, l, cols["ff"], f_bias, seq)
            return (x, fkv, skv, q, bm.reshape(batch, seq, 3 * gw), glogf.reshape(batch, seq, gw),
                    gk.reshape(batch, seq, gw), flogf.reshape(batch, seq, LANES),
                    fcum.reshape(batch, seq, LANES))

        def post_mix(x, o_fox, o_sb, o_hg, p_l):
            x = _out_proj(x, o_fox, o_sb, o_hg, w_out_b, l)
            x, h = _ffn(x, ffn2_norm[l][None, :], *ffn2, l, next_gain=ple_norm[l][None, :])
            return _ple(x, h, w_gate_b, p_l, w_emb_b, l)

        xp, pf_kv, psb_kv, q, bm, glogf, gk, flogf, fcum = pre_mix(xp, B, S, pf_kv, psb_kv)
        cum_t = jnp.swapaxes(fcum[:, :, :hf], 1, 2)
        o_fox = _fox_prompt(q, pf_kv, l, fcum, cum_t, out_gain, hf)
        o_sb = _sb_prompt(q, psb_kv, l, out_gain, hs, hf, hf)
        o_hg, st = _hgrn(bm, glogf, gk, out_gain, hg, fw + sw, None)
        xp = post_mix(xp, o_fox.reshape(B * S, fw), o_sb.reshape(B * S, sw), o_hg.reshape(B * S, gw),
                      p_prompt[l].reshape(B * S, -1))
        logf_p.append(flogf[:, :, :hf])
        state_p.append(st)

        xs, sf_kv, ssb_kv, q, bm, glogf, gk, flogf, fcum = pre_mix(xs, Bs, L, sf_kv, ssb_kv)
        cum_t = jnp.pad(jnp.swapaxes(fcum[:, :, :hf], 1, 2), ((0, 0), (0, 0), (0, LANES - L)))
        o_fox = _fox_sample(page_table, q, sf_kv, fcum, cum_t, cache_fkv, cache_lft, out_gain, l, hf)
        o_sb = _sb_sample(page_table, q, ssb_kv, cache_skv, out_gain, l, hs)

        def pad_tokens(z):
            return jnp.pad(z, ((0, 0), (0, hg_pad - L), (0, 0)))

        o_hg, st = _hgrn(pad_tokens(bm), pad_tokens(glogf), pad_tokens(gk), out_gain, hg, fw + sw,
                         state_hgrn[l].astype(F32))
        xs = post_mix(xs, o_fox.reshape(Bs * L, fw), o_sb.reshape(Bs * L, sw),
                      o_hg[:, :L].reshape(Bs * L, gw), p_sample[l].reshape(Bs * L, -1))
        logf_s.append(flogf[:, :, :hf])
        state_s.append(st)

    return (xp.reshape(B, S, D), xs.reshape(Bs, L, D),
            pf_kv.reshape(B, depth, S, hf, 2 * HEAD_DIM), jnp.stack(logf_p, axis=1),
            psb_kv.reshape(B, depth, S, hs, 2 * HEAD_DIM), jnp.stack(state_p, axis=0),
            sf_kv.reshape(Bs, depth, L, hf, 2 * HEAD_DIM), jnp.stack(logf_s, axis=1),
            ssb_kv.reshape(Bs, depth, L, hs, 2 * HEAD_DIM), jnp.stack(state_s, axis=0))
```
